```python
import math
import jax, jax.numpy as jnp
from jax import lax
import numpy as np

D_MODEL = 1024
BATCH = 8
SEQ = 2048
DEPTH = 2
DEC_BATCH = 32
DEC_SEQ = 8
PAST_LEN = 8192
PAGE_SIZE = 128

N_HEADS = 16
HEAD_DIM = D_MODEL // N_HEADS
Q_BLOCK = 128
SB_BIAS_INIT = -6.0
SSM_GROUP = 16
D_INNER = D_MODEL
N_GROUPS = D_INNER // SSM_GROUP
STATE_DIM = 64
DT_MIN = 0.001
DT_MAX = 0.1
D_FF = 2816
CONV_W = 3
N_ATTN = (DEPTH + 1) // 2
N_SSM = DEPTH // 2
NORM_EPS = 1e-6

kernel_name = "hybrid_stickbreak_s5_convffn_step"


def rms_norm(x, gain):
    xf = x.astype(jnp.float32)
    y = xf * lax.rsqrt(jnp.mean(xf * xf, axis=-1, keepdims=True) + NORM_EPS)
    return (y * gain.astype(jnp.float32)).astype(x.dtype)


def stick_breaking_block(q, k, v, bias, q_start):
    tq, tk = q.shape[1], k.shape[1]
    z = (jnp.einsum("bqhd,bkhd->bhqk", q.astype(jnp.float32), k.astype(jnp.float32)) * (HEAD_DIM ** -0.5)
         + bias.astype(jnp.float32)[None, :, None, None])
    q_pos = q_start + jnp.arange(tq)
    k_pos = jnp.arange(tk)
    causal = k_pos[None, :] < q_pos[:, None]
    log_keep = jnp.where(causal, jax.nn.log_sigmoid(-z), 0.0)
    between = lax.cumsum(log_keep, axis=3, reverse=True) - log_keep
    weights = jnp.where(causal, jnp.exp(jax.nn.log_sigmoid(z) + between), 0.0)
    return jnp.einsum("bhqk,bkhd->bqhd", weights.astype(v.dtype), v)


def stick_breaking_attention(q, k, v, bias, q_offset):
    tq = q.shape[1]
    outs = []
    for start in range(0, tq, Q_BLOCK):
        stop = min(start + Q_BLOCK, tq)
        k_end = q_offset + stop
        outs.append(stick_breaking_block(q[:, start:stop], k[:, :k_end], v[:, :k_end], bias, q_offset + start))
    return jnp.concatenate(outs, axis=1)


def attn_mixer(h, w_qkv, q_gain, k_gain, logit_bias, w_o, past_k, past_v):
    b, s, _ = h.shape
    qkv = (h @ w_qkv).reshape(b, s, 3, N_HEADS, HEAD_DIM)
    q = rms_norm(qkv[:, :, 0], q_gain)
    k = rms_norm(qkv[:, :, 1], k_gain)
    v = qkv[:, :, 2]
    if past_k is None:
        k_all, v_all, offset = k, v, 0
    else:
        k_all = jnp.concatenate([past_k.astype(k.dtype), k], axis=1)
        v_all = jnp.concatenate([past_v.astype(v.dtype), v], axis=1)
        offset = past_k.shape[1]
    o = stick_breaking_attention(q, k_all, v_all, logit_bias, offset)
    y = o.reshape(b, s, N_HEADS * HEAD_DIM) @ w_o
    return y, k, v


def _complex_affine_combine(earlier, later):
    a1r, a1i, b1r, b1i = earlier
    a2r, a2i, b2r, b2i = later
    return (a2r * a1r - a2i * a1i,
            a2r * a1i + a2i * a1r,
            a2r * b1r - a2i * b1i + b2r,
            a2r * b1i + a2i * b1r + b2i)


def s5_mixer(h, w_in, lam_re, lam_im, log_dt, b_re, b_im, c_re, c_im, d_skip, w_glu, s_prev_re, s_prev_im):
    f32 = jnp.float32
    bsz, s, _ = h.shape
    u = (h @ w_in).astype(f32)
    ug = u.reshape(bsz, s, N_GROUPS, SSM_GROUP)
    lr, li = lam_re.astype(f32), lam_im.astype(f32)
    dt = jnp.exp(log_dt.astype(f32))[:, None]
    mag = jnp.exp(lr * dt)
    a_re = mag * jnp.cos(li * dt)
    a_im = mag * jnp.sin(li * dt)
    den = lr * lr + li * li
    zr = ((a_re - 1.0) * lr + a_im * li) / den
    zi = (a_im * lr - (a_re - 1.0) * li) / den
    br, bi = b_re.astype(f32), b_im.astype(f32)
    bbar_re = zr[..., None] * br - zi[..., None] * bi
    bbar_im = zr[..., None] * bi + zi[..., None] * br
    bu_re = jnp.einsum("bsgc,gpc->bsgp", ug, bbar_re)
    bu_im = jnp.einsum("bsgc,gpc->bsgp", ug, bbar_im)
    if s_prev_re is not None:
        pr, pi = s_prev_re.astype(f32), s_prev_im.astype(f32)
        bu_re = bu_re.at[:, 0].add(a_re * pr - a_im * pi)
        bu_im = bu_im.at[:, 0].add(a_re * pi + a_im * pr)
    ar = jnp.broadcast_to(a_re, bu_re.shape)
    ai = jnp.broadcast_to(a_im, bu_im.shape)
    _, _, st_re, st_im = lax.associative_scan(_complex_affine_combine, (ar, ai, bu_re, bu_im), axis=1)
    y = (jnp.einsum("bsgp,gcp->bsgc", st_re, c_re.astype(f32))
         - jnp.einsum("bsgp,gcp->bsgc", st_im, c_im.astype(f32)))
    y = y.reshape(bsz, s, D_INNER) + d_skip.astype(f32) * u
    g = jax.nn.gelu(y).astype(h.dtype)
    ga, gb = jnp.split(g @ w_glu, 2, axis=-1)
    out = ga * jax.nn.sigmoid(gb)
    return out, st_re[:, -1], st_im[:, -1]


def conv_ffn(h, w_up, conv_w, conv_b, w_down, prev):
    s = h.shape[1]
    a, b = jnp.split(h @ w_up, 2, axis=-1)
    padded = jnp.concatenate([prev.astype(a.dtype), a], axis=1)
    c = conv_b + conv_w[0] * padded[:, 0:s]
    for j in range(1, CONV_W):
        c = c + conv_w[j] * padded[:, j:j + s]
    out = (jax.nn.silu(c) * b) @ w_down
    return out, padded[:, s:]


def trunk(x, past_kv, ssm_prev, conv_prev, p):
    new_k, new_v, new_sr, new_si, new_conv = [], [], [], [], []
    for i in range(DEPTH):
        j = i // 2
        h = rms_norm(x, p["norm_mix"][i])
        if i % 2 == 0:
            pk, pv = past_kv[j] if past_kv is not None else (None, None)
            y, k, v = attn_mixer(h, p["attn_w_qkv"][j], p["attn_q_gain"][j], p["attn_k_gain"][j],
                                 p["attn_logit_bias"][j], p["attn_w_o"][j], pk, pv)
            new_k.append(k)
            new_v.append(v)
        else:
            if ssm_prev is not None:
                sr, si = ssm_prev[0][j], ssm_prev[1][j]
            else:
                sr, si = None, None
            y, fr, fi = s5_mixer(h, p["ssm_w_in"][j], p["ssm_lambda_re"][j], p["ssm_lambda_im"][j],
                                 p["ssm_log_dt"][j], p["ssm_b_re"][j], p["ssm_b_im"][j],
                                 p["ssm_c_re"][j], p["ssm_c_im"][j], p["ssm_d"][j], p["ssm_w_glu"][j], sr, si)
            new_sr.append(fr)
            new_si.append(fi)
        x = x + y.astype(x.dtype)
        h = rms_norm(x, p["norm_ffn"][i])
        if conv_prev is not None:
            prev = conv_prev[i]
        else:
            prev = jnp.zeros((x.shape[0], CONV_W - 1, D_FF), x.dtype)
        y, cs = conv_ffn(h, p["ffn_w_up"][i], p["ffn_conv_w"][i], p["ffn_conv_b"][i], p["ffn_w_down"][i], prev)
        x = x + y.astype(x.dtype)
        new_conv.append(cs)
    return (x, jnp.stack(new_k), jnp.stack(new_v), jnp.stack(new_sr), jnp.stack(new_si), jnp.stack(new_conv))


def setup_inputs(seed: int = 0) -> dict:
    key = jax.random.key(seed)
    ks = jax.random.split(key, 32)
    f32 = jnp.float32
    n_pages = PAST_LEN // PAGE_SIZE
    n_used = DEC_BATCH * n_pages
    n_pool = n_used + n_used // 4
    nrm = lambda k, shape, scale: jax.random.normal(k, shape, f32) * scale
    page_table = jax.random.permutation(ks[0], n_pool)[:n_used].reshape(DEC_BATCH, n_pages).astype(jnp.int32)
    lam_im = (math.pi * jnp.arange(STATE_DIM, dtype=f32))[None, None, :] + nrm(ks[14], (N_SSM, N_GROUPS, STATE_DIM), 0.01)
    return {
        "x_prompt": nrm(ks[1], (BATCH, SEQ, D_MODEL), 1.0),
        "x_sample": nrm(ks[2], (DEC_BATCH, DEC_SEQ, D_MODEL), 1.0),
        "cache_k": nrm(ks[3], (N_ATTN, n_pool, PAGE_SIZE, N_HEADS, HEAD_DIM), 1.0),
        "cache_v": nrm(ks[4], (N_ATTN, n_pool, PAGE_SIZE, N_HEADS, HEAD_DIM), 1.0),
        "state_ssm_re": nrm(ks[5], (N_SSM, DEC_BATCH, N_GROUPS, STATE_DIM), 0.1),
        "state_ssm_im": nrm(ks[6], (N_SSM, DEC_BATCH, N_GROUPS, STATE_DIM), 0.1),
        "state_ffn_conv": nrm(ks[7], (DEPTH, DEC_BATCH, CONV_W - 1, D_FF), 1.0),
        "page_table": page_table,
        "norm_mix": 1.0 + nrm(ks[8], (DEPTH, D_MODEL), 0.01),
        "norm_ffn": 1.0 + nrm(ks[9], (DEPTH, D_MODEL), 0.01),
        "attn_w_qkv": nrm(ks[10], (N_ATTN, D_MODEL, 3 * N_HEADS * HEAD_DIM), D_MODEL ** -0.5),
        "attn_q_gain": 1.0 + nrm(ks[11], (N_ATTN, HEAD_DIM), 0.01),
        "attn_k_gain": 1.0 + nrm(ks[12], (N_ATTN, HEAD_DIM), 0.01),
        "attn_logit_bias": SB_BIAS_INIT + nrm(ks[28], (N_ATTN, N_HEADS), 0.1),
        "attn_w_o": nrm(ks[13], (N_ATTN, N_HEADS * HEAD_DIM, D_MODEL), (N_HEADS * HEAD_DIM) ** -0.5),
        "ssm_w_in": nrm(ks[15], (N_SSM, D_MODEL, D_INNER), D_MODEL ** -0.5),
        "ssm_lambda_re": -0.5 + nrm(ks[16], (N_SSM, N_GROUPS, STATE_DIM), 0.01),
        "ssm_lambda_im": lam_im,
        "ssm_log_dt": jax.random.uniform(ks[17], (N_SSM, N_GROUPS), f32, math.log(DT_MIN), math.log(DT_MAX)),
        "ssm_b_re": nrm(ks[18], (N_SSM, N_GROUPS, STATE_DIM, SSM_GROUP), (2 * SSM_GROUP) ** -0.5),
        "ssm_b_im": nrm(ks[19], (N_SSM, N_GROUPS, STATE_DIM, SSM_GROUP), (2 * SSM_GROUP) ** -0.5),
        "ssm_c_re": nrm(ks[20], (N_SSM, N_GROUPS, SSM_GROUP, STATE_DIM), STATE_DIM ** -0.5),
        "ssm_c_im": nrm(ks[21], (N_SSM, N_GROUPS, SSM_GROUP, STATE_DIM), STATE_DIM ** -0.5),
        "ssm_d": nrm(ks[22], (N_SSM, D_INNER), 1.0),
        "ssm_w_glu": nrm(ks[23], (N_SSM, D_INNER, 2 * D_MODEL), D_INNER ** -0.5),
        "ffn_w_up": nrm(ks[24], (DEPTH, D_MODEL, 2 * D_FF), D_MODEL ** -0.5),
        "ffn_conv_w": nrm(ks[25], (DEPTH, CONV_W, D_FF), CONV_W ** -0.5),
        "ffn_conv_b": nrm(ks[26], (DEPTH, D_FF), 0.01),
        "ffn_w_down": nrm(ks[27], (DEPTH, D_FF, D_MODEL), D_FF ** -0.5),
    }


def reference(x_prompt, x_sample, cache_k, cache_v, state_ssm_re, state_ssm_im, state_ffn_conv, page_table,
              norm_mix, norm_ffn, attn_w_qkv, attn_q_gain, attn_k_gain, attn_logit_bias, attn_w_o,
              ssm_w_in, ssm_lambda_re, ssm_lambda_im, ssm_log_dt, ssm_b_re, ssm_b_im, ssm_c_re, ssm_c_im,
              ssm_d, ssm_w_glu, ffn_w_up, ffn_conv_w, ffn_conv_b, ffn_w_down):
    p = {
        "norm_mix": norm_mix, "norm_ffn": norm_ffn,
        "attn_w_qkv": attn_w_qkv, "attn_q_gain": attn_q_gain, "attn_k_gain": attn_k_gain,
        "attn_logit_bias": attn_logit_bias, "attn_w_o": attn_w_o,
        "ssm_w_in": ssm_w_in, "ssm_lambda_re": ssm_lambda_re, "ssm_lambda_im": ssm_lambda_im,
        "ssm_log_dt": ssm_log_dt, "ssm_b_re": ssm_b_re, "ssm_b_im": ssm_b_im,
        "ssm_c_re": ssm_c_re, "ssm_c_im": ssm_c_im, "ssm_d": ssm_d, "ssm_w_glu": ssm_w_glu,
        "ffn_w_up": ffn_w_up, "ffn_conv_w": ffn_conv_w, "ffn_conv_b": ffn_conv_b, "ffn_w_down": ffn_w_down,
    }
    y_prompt, k_prompt, v_prompt, ssm_re_prompt, ssm_im_prompt, conv_prompt = trunk(x_prompt, None, None, None, p)
    db, n_pages = page_table.shape
    past_kv = [(cache_k[j][page_table].reshape(db, n_pages * PAGE_SIZE, N_HEADS, HEAD_DIM),
                cache_v[j][page_table].reshape(db, n_pages * PAGE_SIZE, N_HEADS, HEAD_DIM))
               for j in range(N_ATTN)]
    y_sample, k_sample, v_sample, ssm_re_sample, ssm_im_sample, conv_sample = trunk(
        x_sample, past_kv, (state_ssm_re, state_ssm_im), state_ffn_conv, p)
    return (y_prompt, y_sample, k_prompt, v_prompt, k_sample, v_sample,
            ssm_re_prompt, ssm_im_prompt, ssm_re_sample, ssm_im_sample, conv_prompt, conv_sample)
```

```python
import functools

import jax
import jax.numpy as jnp
from jax import lax
from jax.experimental import pallas as pl
from jax.experimental.pallas import tpu as pltpu

F32 = jnp.float32
BF16 = jnp.bfloat16

D_MODEL = 1024
N_HEADS = 16
HEAD_DIM = 64
PAGE_SIZE = 128
SSM_GROUP = 16
N_GROUPS = D_MODEL // SSM_GROUP
STATE_DIM = 64
D_FF = 2816
NORM_EPS = 1e-6

LANES = 128
SUBLANES = 8
MXU_DIM = 256
VMEM_LIMIT = 56 * 1024 * 1024

HEADS_PER_SLAB = LANES // HEAD_DIM
SSM_BLOCK_IN = MXU_DIM
SSM_GROUPS_PER_BLOCK = SSM_BLOCK_IN // SSM_GROUP
SSM_BLOCK_STATE = SSM_GROUPS_PER_BLOCK * STATE_DIM
N_SSM_BLOCKS = D_MODEL // SSM_BLOCK_IN
SSM_SCAN_LANES = 512
FF_CHUNK = MXU_DIM


def _dot(a, b):
    return jnp.dot(a, b, preferred_element_type=F32)


def _dot_nt(a, b):
    return lax.dot_general(a, b, (((1,), (1,)), ((), ())), preferred_element_type=F32)


def _rms(x, gain):
    return x * lax.rsqrt(jnp.mean(x * x, axis=-1, keepdims=True) + NORM_EPS) * gain


def _resident(shape):
    nd = len(shape)
    return pl.BlockSpec(shape, lambda *_: (0,) * nd, pipeline_mode=pl.Buffered(1))


def _neg_softplus(z):
    return -(jnp.maximum(z, 0.0) + jnp.log(1.0 + jnp.exp(-jnp.abs(z))))


def _qkv_body(x_ref, gm_ref, w_ref, qg_ref, kg_ref, q_ref, k_ref, v_ref, kb_ref, vb_ref):
    h = _rms(x_ref[...], gm_ref[...]).astype(BF16)
    r = lax.broadcasted_iota(jnp.int32, (MXU_DIM, MXU_DIM), 0) // HEAD_DIM
    c = lax.broadcasted_iota(jnp.int32, (MXU_DIM, MXU_DIM), 1) // HEAD_DIM
    seg = (r == c).astype(BF16)

    def head_norm(t, gain):
        sq = (t * t).astype(BF16)
        ms = jnp.concatenate(
            [_dot(sq[:, j * MXU_DIM:(j + 1) * MXU_DIM], seg) for j in range(D_MODEL // MXU_DIM)],
            axis=1) * (1.0 / HEAD_DIM)
        return t * lax.rsqrt(ms + NORM_EPS) * gain

    q = head_norm(_dot(h, w_ref[:, 0:D_MODEL]), qg_ref[...])
    q_ref[...] = (q * (HEAD_DIM ** -0.5)).astype(BF16)
    k = head_norm(_dot(h, w_ref[:, D_MODEL:2 * D_MODEL]), kg_ref[...])
    k_ref[...] = k
    kb_ref[...] = k.astype(BF16)
    v = _dot(h, w_ref[:, 2 * D_MODEL:3 * D_MODEL])
    v_ref[...] = v
    vb_ref[...] = v.astype(BF16)


def _qkv(x2d, g_mix, w_qkv, q_gain, k_gain, tm):
    t = x2d.shape[0]
    row = lambda i: (i, 0)
    tok = pl.BlockSpec((tm, D_MODEL), row)
    return pl.pallas_call(
        _qkv_body,
        grid=(t // tm,),
        in_specs=[tok, _resident((1, D_MODEL)), _resident((D_MODEL, 3 * D_MODEL)),
                  _resident((1, D_MODEL)), _resident((1, D_MODEL))],
        out_specs=[tok, tok, tok, tok, tok],
        out_shape=[jax.ShapeDtypeStruct((t, D_MODEL), BF16),
                   jax.ShapeDtypeStruct((t, D_MODEL), F32),
                   jax.ShapeDtypeStruct((t, D_MODEL), F32),
                   jax.ShapeDtypeStruct((t, D_MODEL), BF16),
                   jax.ShapeDtypeStruct((t, D_MODEL), BF16)],
        compiler_params=pltpu.CompilerParams(
            dimension_semantics=("arbitrary",), vmem_limit_bytes=VMEM_LIMIT),
        name="qkv_proj",
    )(x2d, g_mix, w_qkv, q_gain, k_gain)


def _attn_prompt_body(bias_ref, q_ref, k_ref, v_ref, o_ref, acc_ref, *, tq):
    slab = pl.program_id(1)
    i = pl.program_id(2)
    q = q_ref[...]
    lane = lax.broadcasted_iota(jnp.int32, (tq, LANES), 1)
    low = lane < HEAD_DIM
    q_heads = [jnp.where(low, q, jnp.zeros_like(q)), jnp.where(low, jnp.zeros_like(q), q)]
    bias = [bias_ref[slab * HEADS_PER_SLAB + h] for h in range(HEADS_PER_SLAB)]
    row = lax.broadcasted_iota(jnp.int32, (tq, tq), 0)
    col = lax.broadcasted_iota(jnp.int32, (tq, tq), 1)
    tri = (row >= col).astype(BF16)
    causal = col < row

    acc_ref[...] = jnp.zeros_like(acc_ref)

    def block(j, carry, diag):
        start = pl.multiple_of(j * tq, tq)
        kb = k_ref[pl.ds(start, tq), :]
        vb = v_ref[pl.ds(start, tq), :]
        out = []
        for h in range(HEADS_PER_SLAB):
            z = _dot_nt(q_heads[h], kb) + bias[h]
            lk = _neg_softplus(z)
            if diag:
                lk = jnp.where(causal, lk, 0.0)
            incl = _dot(lk.astype(BF16), tri)
            w = jnp.exp(z + incl + carry[h])
            if diag:
                w = jnp.where(causal, w, 0.0)
            acc_ref[h] += _dot(w.astype(BF16), vb)
            out.append(carry[h] + incl[:, 0:1])
        return tuple(out)

    zero = jnp.zeros((tq, 1), F32)
    carry = block(i, (zero,) * HEADS_PER_SLAB, True)
    lax.fori_loop(0, i, lambda jj, c: block(i - 1 - jj, c, False), carry)
    o_ref[...] = jnp.where(low, acc_ref[0], acc_ref[1]).astype(BF16)


def _attn_prompt(q, k, v, bias, tq):
    b, s, _ = q.shape
    n_slabs = D_MODEL // LANES
    kv_spec = pl.BlockSpec((None, s, LANES), lambda bi, hi, qi: (bi, 0, hi))
    q_spec = pl.BlockSpec((None, tq, LANES), lambda bi, hi, qi: (bi, qi, hi))
    return pl.pallas_call(
        functools.partial(_attn_prompt_body, tq=tq),
        grid=(b, n_slabs, s // tq),
        in_specs=[pl.BlockSpec(memory_space=pltpu.SMEM), q_spec, kv_spec, kv_spec],
        out_specs=q_spec,
        out_shape=jax.ShapeDtypeStruct((b, s, D_MODEL), BF16),
        scratch_shapes=[pltpu.VMEM((HEADS_PER_SLAB, tq, LANES), F32)],
        compiler_params=pltpu.CompilerParams(
            dimension_semantics=("arbitrary", "arbitrary", "arbitrary"),
            vmem_limit_bytes=VMEM_LIMIT),
        name="attn_prompt",
    )(bias, q, k, v)


def _attn_sample_body(pt_ref, q_ref, kn_ref, vn_ref, bias_ref, kp_ref, vp_ref, o_ref,
                      qbd_ref, kpad_ref, vpad_ref, acc_ref, carry_ref, *, n_q):
    del pt_ref
    p = pl.program_id(1)
    n_cols = N_HEADS * n_q
    row_k = lax.broadcasted_iota(jnp.int32, (PAGE_SIZE, n_cols), 0)
    col_q = lax.broadcasted_iota(jnp.int32, (PAGE_SIZE, n_cols), 1) % n_q
    r = lax.broadcasted_iota(jnp.int32, (PAGE_SIZE, PAGE_SIZE), 0)
    c = lax.broadcasted_iota(jnp.int32, (PAGE_SIZE, PAGE_SIZE), 1)
    tri = (c >= r).astype(BF16)

    @pl.when(p == 0)
    def _():
        q_rep = jnp.concatenate([q_ref[...].astype(F32)] * N_HEADS, axis=0)
        rh = lax.broadcasted_iota(jnp.int32, (n_cols, D_MODEL), 0) // n_q
        ch = lax.broadcasted_iota(jnp.int32, (n_cols, D_MODEL), 1) // HEAD_DIM
        qbd_ref[...] = jnp.where(rh == ch, q_rep, 0.0).astype(BF16)
        kpad_ref[...] = jnp.zeros_like(kpad_ref)
        vpad_ref[...] = jnp.zeros_like(vpad_ref)
        kpad_ref[0:n_q, :] = kn_ref[...]
        vpad_ref[0:n_q, :] = vn_ref[...]
        acc_ref[...] = jnp.zeros_like(acc_ref)
        carry_ref[...] = jnp.zeros_like(carry_ref)

    def process(k_f32, v_f32, new):
        z = _dot_nt(k_f32.astype(BF16), qbd_ref[...]) + bias_ref[...]
        lk = _neg_softplus(z)
        if new:
            visible = row_k < col_q
            lk = jnp.where(visible, lk, 0.0)
        incl = _dot(tri, lk.astype(BF16))
        w = jnp.exp(z + incl + carry_ref[0:1, :])
        if new:
            w = jnp.where(visible, w, 0.0)
        acc_ref[...] += lax.dot_general(w.astype(BF16), v_f32.astype(BF16),
                                        (((0,), (0,)), ((), ())), preferred_element_type=F32)
        carry_ref[...] = carry_ref[...] + incl[0:1, :]

    @pl.when(p == 0)
    def _():
        process(kpad_ref[...], vpad_ref[...], True)

    @pl.when(p > 0)
    def _():
        process(kp_ref[...], vp_ref[...], False)

    @pl.when(p == pl.num_programs(1) - 1)
    def _():
        rh = lax.broadcasted_iota(jnp.int32, (n_cols, D_MODEL), 0) // n_q
        ch = lax.broadcasted_iota(jnp.int32, (n_cols, D_MODEL), 1) // HEAD_DIM
        own = jnp.where(rh == ch, acc_ref[...], 0.0)
        out = own[0:n_q, :]
        for h in range(1, N_HEADS):
            out = out + own[h * n_q:(h + 1) * n_q, :]
        o_ref[...] = out.astype(BF16)


def _attn_sample(q, k_new, v_new, bias_cols, cache_k, cache_v, page_table):
    db, n_q, _ = q.shape
    n_pages = page_table.shape[1]
    n_cols = N_HEADS * n_q

    def page_idx(bi, pi, pt):
        return (pt[bi, n_pages - 1 - jnp.maximum(pi - 1, 0)], 0, 0)

    per_seq = lambda bi, pi, pt: (bi, 0, 0)
    grid_spec = pltpu.PrefetchScalarGridSpec(
        num_scalar_prefetch=1,
        grid=(db, n_pages + 1),
        in_specs=[pl.BlockSpec((None, n_q, D_MODEL), per_seq),
                  pl.BlockSpec((None, n_q, D_MODEL), per_seq),
                  pl.BlockSpec((None, n_q, D_MODEL), per_seq),
                  pl.BlockSpec((1, n_cols), lambda bi, pi, pt: (0, 0)),
                  pl.BlockSpec((None, PAGE_SIZE, D_MODEL), page_idx),
                  pl.BlockSpec((None, PAGE_SIZE, D_MODEL), page_idx)],
        out_specs=pl.BlockSpec((None, n_q, D_MODEL), per_seq),
        scratch_shapes=[pltpu.VMEM((n_cols, D_MODEL), BF16),
                        pltpu.VMEM((PAGE_SIZE, D_MODEL), F32),
                        pltpu.VMEM((PAGE_SIZE, D_MODEL), F32),
                        pltpu.VMEM((n_cols, D_MODEL), F32),
                        pltpu.VMEM((SUBLANES, n_cols), F32)])
    return pl.pallas_call(
        functools.partial(_attn_sample_body, n_q=n_q),
        grid_spec=grid_spec,
        out_shape=jax.ShapeDtypeStruct((db, n_q, D_MODEL), BF16),
        compiler_params=pltpu.CompilerParams(
            dimension_semantics=("arbitrary", "arbitrary"), vmem_limit_bytes=VMEM_LIMIT),
        name="attn_sample",
    )(page_table, q, k_new, v_new, bias_cols, cache_k, cache_v)


def _ffn_body(*refs, with_wo, short_seq, tiles_per_seq, seq_len):
    refs = list(refs)
    x_ref = refs.pop(0)
    if with_wo:
        o_ref, wo_ref = refs.pop(0), refs.pop(0)
    g_ref, wup_ref, cw_ref, cb_ref, wdn_ref = [refs.pop(0) for _ in range(5)]
    if short_seq:
        p1_ref, p2_ref = refs.pop(0), refs.pop(0)
    out_ref, conv_ref, a_scr, g_scr = refs
    tm = x_ref.shape[0]

    x1 = x_ref[...]
    if with_wo:
        x1 = x1 + _dot(o_ref[...], wo_ref[...])
    h = _rms(x1, g_ref[...]).astype(BF16)

    if short_seq:
        a_scr[0:SUBLANES, :] = jnp.zeros((SUBLANES, D_FF), F32)
    else:
        @pl.when(pl.program_id(0) % tiles_per_seq == 0)
        def _():
            a_scr[0:SUBLANES, :] = jnp.zeros((SUBLANES, D_FF), F32)

    a_scr[SUBLANES:SUBLANES + tm, :] = _dot(h, wup_ref[:, 0:D_FF])

    row = lax.broadcasted_iota(jnp.int32, (tm, FF_CHUNK), 0)
    for c in range(D_FF // FF_CHUNK):
        cs = slice(c * FF_CHUNK, (c + 1) * FF_CHUNK)
        a0 = a_scr[SUBLANES:SUBLANES + tm, cs]
        a1 = a_scr[SUBLANES - 1:SUBLANES - 1 + tm, cs]
        a2 = a_scr[SUBLANES - 2:SUBLANES - 2 + tm, cs]
        if short_seq:
            pos = row % seq_len
            a1 = jnp.where(pos < 1, p1_ref[:, cs], a1)
            a2 = jnp.where(pos < 2, p2_ref[:, cs], a2)
        b = _dot(h, wup_ref[:, D_FF + c * FF_CHUNK:D_FF + (c + 1) * FF_CHUNK])
        cc = cb_ref[:, cs] + cw_ref[0:1, cs] * a2 + cw_ref[1:2, cs] * a1 + cw_ref[2:3, cs] * a0
        g_scr[:, cs] = (cc * jax.nn.sigmoid(cc) * b).astype(BF16)

    tail = a_scr[tm:tm + SUBLANES, :]
    if short_seq:
        conv_ref[...] = a_scr[SUBLANES:SUBLANES + tm, :]
    else:
        conv_ref[...] = tail
        a_scr[0:SUBLANES, :] = tail
    out_ref[...] = x1 + _dot(g_scr[...], wdn_ref[...])


def _ffn(x2d, o2d, w_o, g_ffn, w_up, conv_w, conv_b, w_down, seq_len, tm, prev_rows=None):
    t = x2d.shape[0]
    with_wo = o2d is not None
    short_seq = prev_rows is not None
    n_tiles = t // tm
    row = lambda i: (i, 0)
    tok = pl.BlockSpec((tm, D_MODEL), row)
    args, specs = [x2d], [tok]
    if with_wo:
        args += [o2d, w_o]
        specs += [tok, _resident((D_MODEL, D_MODEL))]
    args += [g_ffn, w_up, conv_w, conv_b, w_down]
    specs += [_resident((1, D_MODEL)), _resident((D_MODEL, 2 * D_FF)), _resident((3, D_FF)),
              _resident((1, D_FF)), _resident((D_FF, D_MODEL))]
    if short_seq:
        args += list(prev_rows)
        specs += [pl.BlockSpec((tm, D_FF), row)] * 2
        conv_shape, conv_spec = (t, D_FF), pl.BlockSpec((tm, D_FF), row)
        tiles_per_seq = 1
    else:
        conv_shape = (n_tiles, SUBLANES, D_FF)
        conv_spec = pl.BlockSpec((None, SUBLANES, D_FF), lambda i: (i, 0, 0))
        tiles_per_seq = seq_len // tm
    return pl.pallas_call(
        functools.partial(_ffn_body, with_wo=with_wo, short_seq=short_seq,
                          tiles_per_seq=tiles_per_seq, seq_len=seq_len),
        grid=(n_tiles,),
        in_specs=specs,
        out_specs=[tok, conv_spec],
        out_shape=[jax.ShapeDtypeStruct((t, D_MODEL), F32),
                   jax.ShapeDtypeStruct(conv_shape, F32)],
        scratch_shapes=[pltpu.VMEM((tm + SUBLANES, D_FF), F32), pltpu.VMEM((tm, D_FF), BF16)],
        compiler_params=pltpu.CompilerParams(
            dimension_semantics=("arbitrary",), vmem_limit_bytes=VMEM_LIMIT),
        name="conv_ffn",
    )(*args)


def _s5_body(x_ref, g_ref, win_ref, bd_ref, cd_ref, are_ref, aim_ref, dskip_ref, wglu_ref, s0_ref,
             xo_ref, sfin_ref, xt_scr, st_scr, bu_scr, y_scr, *, tt):
    ti = pl.program_id(1)
    n_seq = SUBLANES

    @pl.when(ti == 0)
    def _():
        st_scr[...] = s0_ref[...]

    n_slabs = D_MODEL // LANES
    for b in range(n_seq):
        for j in range(n_slabs):
            xt_scr[j, pl.ds(b, tt, stride=n_seq), :] = x_ref[b, :, j * LANES:(j + 1) * LANES]
    xt = jnp.concatenate([xt_scr[j] for j in range(n_slabs)], axis=1)
    u = _dot(_rms(xt, g_ref[...]).astype(BF16), win_ref[...])
    ub = u.astype(BF16)

    for k in range(N_SSM_BLOCKS):
        bu_scr[...] = _dot(ub[:, k * SSM_BLOCK_IN:(k + 1) * SSM_BLOCK_IN], bd_ref[k])
        for lc in range(SSM_BLOCK_STATE // SSM_SCAN_LANES):
            re_l = slice(lc * SSM_SCAN_LANES, (lc + 1) * SSM_SCAN_LANES)
            im_l = slice(SSM_BLOCK_STATE + lc * SSM_SCAN_LANES,
                         SSM_BLOCK_STATE + (lc + 1) * SSM_SCAN_LANES)
            base = k * 2 * SSM_BLOCK_STATE
            st_re = slice(base + re_l.start, base + re_l.stop)
            st_im = slice(base + im_l.start, base + im_l.stop)
            ar = jnp.broadcast_to(are_ref[k, :, re_l], (n_seq, SSM_SCAN_LANES))
            ai = jnp.broadcast_to(aim_ref[k, :, re_l], (n_seq, SSM_SCAN_LANES))

            def step(t, s, ar=ar, ai=ai, re_l=re_l, im_l=im_l):
                sr, si = s
                r0 = pl.multiple_of(t * n_seq, n_seq)
                nr = ar * sr - ai * si + bu_scr[pl.ds(r0, n_seq), re_l]
                ni = ar * si + ai * sr + bu_scr[pl.ds(r0, n_seq), im_l]
                bu_scr[pl.ds(r0, n_seq), re_l] = nr
                bu_scr[pl.ds(r0, n_seq), im_l] = ni
                return nr, ni

            sr, si = lax.fori_loop(0, tt, step, (st_scr[:, st_re], st_scr[:, st_im]), unroll=4)
            st_scr[:, st_re] = sr
            st_scr[:, st_im] = si
        y_scr[:, k * SSM_BLOCK_IN:(k + 1) * SSM_BLOCK_IN] = _dot(bu_scr[...].astype(BF16), cd_ref[k])

    y = y_scr[...] + dskip_ref[...] * u
    gg = _dot(jax.nn.gelu(y).astype(BF16), wglu_ref[...])
    x_new = xt + gg[:, 0:D_MODEL] * jax.nn.sigmoid(gg[:, D_MODEL:2 * D_MODEL])
    for j in range(n_slabs):
        xt_scr[j] = x_new[:, j * LANES:(j + 1) * LANES]
    for b in range(n_seq):
        for j in range(n_slabs):
            xo_ref[b, :, j * LANES:(j + 1) * LANES] = xt_scr[j, pl.ds(b, tt, stride=n_seq), :]

    @pl.when(ti == pl.num_programs(1) - 1)
    def _():
        sfin_ref[...] = st_scr[...]


def _s5(x, g_mix, w_in, bd, cd, a_re, a_im, d_skip, w_glu, s0, tt):
    b, s, _ = x.shape
    n_state = N_SSM_BLOCKS * 2 * SSM_BLOCK_STATE
    x_spec = pl.BlockSpec((SUBLANES, tt, D_MODEL), lambda bi, ti: (bi, ti, 0))
    s_spec = pl.BlockSpec((None, SUBLANES, n_state), lambda bi, ti: (bi, 0, 0))
    rows = tt * SUBLANES
    return pl.pallas_call(
        functools.partial(_s5_body, tt=tt),
        grid=(b // SUBLANES, s // tt),
        in_specs=[x_spec, _resident((1, D_MODEL)), _resident((D_MODEL, D_MODEL)),
                  _resident(bd.shape), _resident(cd.shape), _resident(a_re.shape),
                  _resident(a_im.shape), _resident((1, D_MODEL)),
                  _resident((D_MODEL, 2 * D_MODEL)), s_spec],
        out_specs=[x_spec, s_spec],
        out_shape=[jax.ShapeDtypeStruct((b, s, D_MODEL), F32),
                   jax.ShapeDtypeStruct((b // SUBLANES, SUBLANES, n_state), F32)],
        scratch_shapes=[pltpu.VMEM((D_MODEL // LANES, rows, LANES), F32),
                        pltpu.VMEM((SUBLANES, n_state), F32),
                        pltpu.VMEM((rows, 2 * SSM_BLOCK_STATE), F32),
                        pltpu.VMEM((rows, D_MODEL), F32)],
        compiler_params=pltpu.CompilerParams(
            dimension_semantics=("arbitrary", "arbitrary"), vmem_limit_bytes=VMEM_LIMIT),
        name="s5_layer",
    )(x, g_mix, w_in, bd, cd, a_re, a_im, d_skip, w_glu, s0)


def _s5_tables(lam_re, lam_im, log_dt, b_re, b_im, c_re, c_im):
    dt = jnp.exp(log_dt)[:, None]
    mag = jnp.exp(lam_re * dt)
    a_re = mag * jnp.cos(lam_im * dt)
    a_im = mag * jnp.sin(lam_im * dt)
    den = lam_re * lam_re + lam_im * lam_im
    zr = ((a_re - 1.0) * lam_re + a_im * lam_im) / den
    zi = (a_im * lam_re - (a_re - 1.0) * lam_im) / den
    bbar_re = zr[..., None] * b_re - zi[..., None] * b_im
    bbar_im = zr[..., None] * b_im + zi[..., None] * b_re
    eye = jnp.eye(SSM_GROUPS_PER_BLOCK, dtype=F32)

    def pack_in(m):
        m = m.reshape(N_SSM_BLOCKS, SSM_GROUPS_PER_BLOCK, STATE_DIM, SSM_GROUP)
        full = jnp.einsum("kgpc,gh->kgchp", m, eye)
        return full.reshape(N_SSM_BLOCKS, SSM_BLOCK_IN, SSM_BLOCK_STATE)

    def pack_out(m):
        m = m.reshape(N_SSM_BLOCKS, SSM_GROUPS_PER_BLOCK, SSM_GROUP, STATE_DIM)
        full = jnp.einsum("kgcp,gh->kgphc", m, eye)
        return full.reshape(N_SSM_BLOCKS, SSM_BLOCK_STATE, SSM_BLOCK_IN)

    bd = jnp.concatenate([pack_in(bbar_re), pack_in(bbar_im)], axis=2).astype(BF16)
    cd = jnp.concatenate([pack_out(c_re), pack_out(-c_im)], axis=1).astype(BF16)
    shape = (N_SSM_BLOCKS, 1, SSM_BLOCK_STATE)
    return bd, cd, a_re.reshape(shape), a_im.reshape(shape)


def _pack_state(s_re, s_im):
    n = s_re.shape[0]
    packed = jnp.concatenate([s_re.reshape(n, N_SSM_BLOCKS, SSM_BLOCK_STATE),
                              s_im.reshape(n, N_SSM_BLOCKS, SSM_BLOCK_STATE)], axis=2)
    return packed.reshape(n // SUBLANES, SUBLANES, -1)


def _unpack_state(packed, n):
    s = packed.reshape(n, N_SSM_BLOCKS, 2, SSM_BLOCK_STATE)
    return (s[:, :, 0].reshape(n, N_GROUPS, STATE_DIM), s[:, :, 1].reshape(n, N_GROUPS, STATE_DIM))


def _trunk(x, w, tables, *, tm, tq, tt, past=None, ssm_prev=None, conv_prev=None):
    b, s, _ = x.shape
    t = b * s
    x2d = x.reshape(t, D_MODEL)

    q, k, v, kb, vb = _qkv(x2d, w["norm_mix"][0], w["w_qkv"], w["q_gain"], w["k_gain"], tm)
    shape3 = (b, s, D_MODEL)
    if past is None:
        o = _attn_prompt(q.reshape(shape3), kb.reshape(shape3), vb.reshape(shape3), w["bias"], tq)
    else:
        cache_k, cache_v, page_table = past
        o = _attn_sample(q.reshape(shape3), k.reshape(shape3), v.reshape(shape3), w["bias_cols"],
                         cache_k, cache_v, page_table)
    prev0 = None if conv_prev is None else conv_prev[0]
    x2d, conv0 = _ffn(x2d, o.reshape(t, D_MODEL), w["w_o"], w["norm_ffn"][0], w["w_up"][0],
                      w["conv_w"][0], w["conv_b"][0], w["w_down"][0], s, tm, prev0)

    bd, cd, a_re, a_im = tables
    if ssm_prev is None:
        s0 = jnp.zeros((b // SUBLANES, SUBLANES, N_SSM_BLOCKS * 2 * SSM_BLOCK_STATE), F32)
    else:
        s0 = _pack_state(*ssm_prev)
    x3, s_fin = _s5(x2d.reshape(shape3), w["norm_mix"][1], w["w_in"], bd, cd, a_re, a_im,
                    w["d_skip"], w["w_glu"], s0, tt)
    s_re, s_im = _unpack_state(s_fin, b)
    prev1 = None if conv_prev is None else conv_prev[1]
    y2d, conv1 = _ffn(x3.reshape(t, D_MODEL), None, None, w["norm_ffn"][1], w["w_up"][1],
                      w["conv_w"][1], w["conv_b"][1], w["w_down"][1], s, tm, prev1)

    def conv_state(c):
        if conv_prev is None:
            per_seq = c.reshape(b, s // tm, SUBLANES, D_FF)[:, -1]
        else:
            per_seq = c.reshape(b, s, D_FF)
        return per_seq[:, -2:]

    heads = (b, s, N_HEADS, HEAD_DIM)
    return (y2d.reshape(shape3), k.reshape(heads)[None], v.reshape(heads)[None], s_re[None], s_im[None],
            jnp.stack([conv_state(conv0), conv_state(conv1)]))


def _short_seq_prev_rows(prev, seq_len):
    bsz = prev.shape[0]
    zeros = jnp.zeros((bsz, seq_len - 2, D_FF), F32)
    p2 = jnp.concatenate([prev, zeros], axis=1)
    p1 = jnp.concatenate([prev[:, 1:2], jnp.zeros((bsz, seq_len - 1, D_FF), F32)], axis=1)
    return p1.reshape(bsz * seq_len, D_FF), p2.reshape(bsz * seq_len, D_FF)


def kernel(x_prompt, x_sample, cache_k, cache_v, state_ssm_re, state_ssm_im, state_ffn_conv, page_table,
           norm_mix, norm_ffn, attn_w_qkv, attn_q_gain, attn_k_gain, attn_logit_bias, attn_w_o,
           ssm_w_in, ssm_lambda_re, ssm_lambda_im, ssm_log_dt, ssm_b_re, ssm_b_im, ssm_c_re, ssm_c_im,
           ssm_d, ssm_w_glu, ffn_w_up, ffn_conv_w, ffn_conv_b, ffn_w_down):
    dec_batch, dec_seq, _ = x_sample.shape
    w = {
        "norm_mix": norm_mix[:, None, :], "norm_ffn": norm_ffn[:, None, :],
        "w_qkv": attn_w_qkv[0].astype(BF16),
        "q_gain": jnp.tile(attn_q_gain[0], N_HEADS)[None], "k_gain": jnp.tile(attn_k_gain[0], N_HEADS)[None],
        "bias": attn_logit_bias[0],
        "bias_cols": jnp.repeat(attn_logit_bias[0], dec_seq)[None],
        "w_o": attn_w_o[0].astype(BF16),
        "w_in": ssm_w_in[0].astype(BF16), "d_skip": ssm_d[0][None], "w_glu": ssm_w_glu[0].astype(BF16),
        "w_up": ffn_w_up.astype(BF16), "conv_w": ffn_conv_w, "conv_b": ffn_conv_b[:, None, :],
        "w_down": ffn_w_down.astype(BF16),
    }
    tables = _s5_tables(ssm_lambda_re[0], ssm_lambda_im[0], ssm_log_dt[0], ssm_b_re[0], ssm_b_im[0],
                        ssm_c_re[0], ssm_c_im[0])

    y_p, k_p, v_p, sr_p, si_p, conv_p = _trunk(x_prompt, w, tables, tm=512, tq=256, tt=64)

    n_pool = cache_k.shape[1]
    past = (cache_k[0].reshape(n_pool, PAGE_SIZE, D_MODEL), cache_v[0].reshape(n_pool, PAGE_SIZE, D_MODEL),
            page_table)
    conv_prev = [_short_seq_prev_rows(state_ffn_conv[i], dec_seq) for i in range(state_ffn_conv.shape[0])]
    y_s, k_s, v_s, sr_s, si_s, conv_s = _trunk(
        x_sample, w, tables, tm=dec_batch * dec_seq, tq=None, tt=dec_seq,
        past=past, ssm_prev=(state_ssm_re[0], state_ssm_im[0]), conv_prev=conv_prev)

    return (y_p, y_s, k_p, v_p, k_s, v_s, sr_p, si_p, sr_s, si_s, conv_p, conv_s)
```

```python
import functools

import jax
import jax.numpy as jnp
from jax import lax
from jax.experimental import pallas as pl
from jax.experimental.pallas import tpu as pltpu

F32 = jnp.float32
BF16 = jnp.bfloat16

D_MODEL = 1024
N_HEADS = 16
HEAD_DIM = 64
PAGE_SIZE = 128
SSM_GROUP = 16
N_GROUPS = D_MODEL // SSM_GROUP
STATE_DIM = 64
D_FF = 2816
NORM_EPS = 1e-6

LANES = 128
SUBLANES = 8
MXU_DIM = 256
VMEM_LIMIT = 56 * 1024 * 1024

HEADS_PER_SLAB = LANES // HEAD_DIM
SSM_BLOCK_IN = MXU_DIM
SSM_GROUPS_PER_BLOCK = SSM_BLOCK_IN // SSM_GROUP
SSM_BLOCK_STATE = SSM_GROUPS_PER_BLOCK * STATE_DIM
N_SSM_BLOCKS = D_MODEL // SSM_BLOCK_IN
SSM_SCAN_LANES = 512
FF_CHUNK = MXU_DIM


def _dot(a, b):
    return jnp.dot(a, b, preferred_element_type=F32)


def _dot_nt(a, b):
    return lax.dot_general(a, b, (((1,), (1,)), ((), ())), preferred_element_type=F32)


def _rms(x, gain):
    return x * lax.rsqrt(jnp.mean(x * x, axis=-1, keepdims=True) + NORM_EPS) * gain


def _resident(shape):
    nd = len(shape)
    return pl.BlockSpec(shape, lambda *_: (0,) * nd, pipeline_mode=pl.Buffered(1))


def _neg_softplus(z):
    return -(jnp.maximum(z, 0.0) + jnp.log(1.0 + jnp.exp(-jnp.abs(z))))


def _head_norm_rows(t, gain):
    r = lax.broadcasted_iota(jnp.int32, (MXU_DIM, MXU_DIM), 0) // HEAD_DIM
    c = lax.broadcasted_iota(jnp.int32, (MXU_DIM, MXU_DIM), 1) // HEAD_DIM
    seg = (r == c).astype(BF16)
    sq = (t * t).astype(BF16)
    ms = jnp.concatenate(
        [_dot(sq[:, j * MXU_DIM:(j + 1) * MXU_DIM], seg) for j in range(D_MODEL // MXU_DIM)],
        axis=1) * (1.0 / HEAD_DIM)
    return t * lax.rsqrt(ms + NORM_EPS) * gain


def _qkv_body(x_ref, gm_ref, wq_ref, wk_ref, wv_ref, qg_ref, kg_ref, q_ref, k_ref, v_ref):
    h = _rms(x_ref[...], gm_ref[...]).astype(BF16)
    q = _head_norm_rows(_dot(h, wq_ref[...]), qg_ref[...])
    q_ref[...] = (q * (HEAD_DIM ** -0.5)).astype(BF16)
    k_ref[...] = _head_norm_rows(_dot(h, wk_ref[...]), kg_ref[...])
    v_ref[...] = _dot(h, wv_ref[...])


def _qkv(x2d, g_mix, wq, wk, wv, q_gain, k_gain, tm):
    t = x2d.shape[0]
    tok = pl.BlockSpec((tm, D_MODEL), lambda i: (i, 0))
    wspec = _resident((D_MODEL, D_MODEL))
    return pl.pallas_call(
        _qkv_body,
        grid=(t // tm,),
        in_specs=[tok, _resident((1, D_MODEL)), wspec, wspec, wspec,
                  _resident((1, D_MODEL)), _resident((1, D_MODEL))],
        out_specs=[tok, tok, tok],
        out_shape=[jax.ShapeDtypeStruct((t, D_MODEL), BF16),
                   jax.ShapeDtypeStruct((t, D_MODEL), F32),
                   jax.ShapeDtypeStruct((t, D_MODEL), F32)],
        compiler_params=pltpu.CompilerParams(
            dimension_semantics=("arbitrary",), vmem_limit_bytes=VMEM_LIMIT),
        name="qkv_proj",
    )(x2d, g_mix, wq, wk, wv, q_gain, k_gain)


def _qkv_t_body(x_ref, gm_ref, wq_ref, wkt_ref, wvt_ref, qg_ref, kgt_ref,
                q_ref, kt_ref, vt_ref, ktb_ref, vtb_ref):
    tm = x_ref.shape[0]
    h = _rms(x_ref[...], gm_ref[...]).astype(BF16)
    q = _head_norm_rows(_dot(h, wq_ref[...]), qg_ref[...])
    q_ref[...] = (q * (HEAD_DIM ** -0.5)).astype(BF16)
    k3 = _dot_nt(wkt_ref[...], h).reshape(N_HEADS, HEAD_DIM, tm)
    ms = jnp.mean(k3 * k3, axis=1, keepdims=True)
    kt = (k3 * lax.rsqrt(ms + NORM_EPS) * kgt_ref[...][None]).reshape(D_MODEL, tm)
    kt_ref[...] = kt
    ktb_ref[...] = kt.astype(BF16)
    vt = _dot_nt(wvt_ref[...], h)
    vt_ref[...] = vt
    vtb_ref[...] = vt.astype(BF16)


def _qkv_t(x2d, g_mix, wq, wkt, wvt, q_gain, k_gain_col, seq_len, tm):
    t = x2d.shape[0]
    tiles_per_seq = seq_len // tm
    tok = pl.BlockSpec((tm, D_MODEL), lambda i: (i, 0))
    feat = pl.BlockSpec((None, D_MODEL, tm), lambda i: (i // tiles_per_seq, 0, i % tiles_per_seq))
    wspec = _resident((D_MODEL, D_MODEL))
    shape_t = (t // seq_len, D_MODEL, seq_len)
    return pl.pallas_call(
        _qkv_t_body,
        grid=(t // tm,),
        in_specs=[tok, _resident((1, D_MODEL)), wspec, wspec, wspec,
                  _resident((1, D_MODEL)), _resident((HEAD_DIM, 1))],
        out_specs=[tok, feat, feat, feat, feat],
        out_shape=[jax.ShapeDtypeStruct((t, D_MODEL), BF16),
                   jax.ShapeDtypeStruct(shape_t, F32),
                   jax.ShapeDtypeStruct(shape_t, F32),
                   jax.ShapeDtypeStruct(shape_t, BF16),
                   jax.ShapeDtypeStruct(shape_t, BF16)],
        compiler_params=pltpu.CompilerParams(
            dimension_semantics=("arbitrary",), vmem_limit_bytes=VMEM_LIMIT),
        name="qkv_proj_t",
    )(x2d, g_mix, wq, wkt, wvt, q_gain, k_gain_col)


def _attn_prompt_body(bias_ref, tri_ref, q_ref, kt_ref, vt_ref, o_ref, acc_ref, *, tq, n_slabs):
    grp = pl.program_id(1)
    i = pl.program_id(2)
    n_heads = n_slabs * HEADS_PER_SLAB
    low = lax.broadcasted_iota(jnp.int32, (tq, LANES), 1) < HEAD_DIM
    q_heads = []
    for s in range(n_slabs):
        q = q_ref[:, s * LANES:(s + 1) * LANES]
        q_heads += [jnp.where(low, q, jnp.zeros_like(q)), jnp.where(low, jnp.zeros_like(q), q)]
    bias = [bias_ref[grp * n_heads + h] for h in range(n_heads)]
    row = lax.broadcasted_iota(jnp.int32, (tq, tq), 0)
    col = lax.broadcasted_iota(jnp.int32, (tq, tq), 1)
    causal = col < row

    acc_ref[...] = jnp.zeros_like(acc_ref)

    def block(j, carry, diag):
        keys = pl.ds(pl.multiple_of(j * tq, tq), tq)
        out = []
        for h in range(n_heads):
            rows = slice((h // HEADS_PER_SLAB) * LANES, (h // HEADS_PER_SLAB + 1) * LANES)
            z = _dot(q_heads[h], kt_ref[rows, keys]) + bias[h]
            lk = _neg_softplus(z)
            if diag:
                lk = jnp.where(causal, lk, 0.0)
            incl = _dot(lk.astype(BF16), tri_ref[...])
            w = jnp.exp(z + incl + carry[h])
            if diag:
                w = jnp.where(causal, w, 0.0)
            acc_ref[h] += _dot_nt(w.astype(BF16), vt_ref[rows, keys])
            out.append(carry[h] + incl[:, 0:1])
        return tuple(out)

    zero = jnp.zeros((tq, 1), F32)
    carry = block(i, (zero,) * n_heads, True)
    lax.fori_loop(0, i, lambda jj, c: block(i - 1 - jj, c, False), carry)
    for s in range(n_slabs):
        o_ref[:, s * LANES:(s + 1) * LANES] = jnp.where(
            low, acc_ref[HEADS_PER_SLAB * s], acc_ref[HEADS_PER_SLAB * s + 1]).astype(BF16)


def _attn_prompt(q, kt, vt, bias, tri, tq, n_slabs):
    b, s, _ = q.shape
    width = n_slabs * LANES
    kv_spec = pl.BlockSpec((None, width, s), lambda bi, gi, qi: (bi, gi, 0))
    q_spec = pl.BlockSpec((None, tq, width), lambda bi, gi, qi: (bi, qi, gi))
    return pl.pallas_call(
        functools.partial(_attn_prompt_body, tq=tq, n_slabs=n_slabs),
        grid=(b, D_MODEL // width, s // tq),
        in_specs=[pl.BlockSpec(memory_space=pltpu.SMEM),
                  pl.BlockSpec((tq, tq), lambda bi, gi, qi: (0, 0)), q_spec, kv_spec, kv_spec],
        out_specs=q_spec,
        out_shape=jax.ShapeDtypeStruct((b, s, D_MODEL), BF16),
        scratch_shapes=[pltpu.VMEM((n_slabs * HEADS_PER_SLAB, tq, LANES), F32)],
        compiler_params=pltpu.CompilerParams(
            dimension_semantics=("arbitrary", "arbitrary", "arbitrary"),
            vmem_limit_bytes=VMEM_LIMIT),
        name="attn_prompt",
    )(bias, tri, q, kt, vt)


def _attn_sample_body(pt_ref, q_ref, kn_ref, vn_ref, bias_ref, tri_ref, *rest, n_q, pps):
    del pt_ref
    kt_refs, vt_refs = rest[:pps], rest[pps:2 * pps]
    o_ref, qbd_ref, kt_scr, vt_scr, w_scr, acc_ref, carry_ref = rest[2 * pps:]
    step = pl.program_id(1)
    n_rows = N_HEADS * n_q
    window = pps * PAGE_SIZE

    @pl.when(step == 0)
    def _():
        q_rep = jnp.concatenate([q_ref[...].astype(F32)] * N_HEADS, axis=0)
        rh = lax.broadcasted_iota(jnp.int32, (n_rows, D_MODEL), 0) // n_q
        ch = lax.broadcasted_iota(jnp.int32, (n_rows, D_MODEL), 1) // HEAD_DIM
        qbd = jnp.where(rh == ch, q_rep, 0.0).astype(BF16)
        qbd_ref[...] = qbd
        pad = jnp.zeros((PAGE_SIZE - n_q, D_MODEL), F32)
        kn = jnp.concatenate([kn_ref[...], pad], axis=0).astype(BF16)
        vn = jnp.concatenate([vn_ref[...], pad], axis=0).astype(BF16)
        z = _dot_nt(qbd, kn) + bias_ref[...]
        key = lax.broadcasted_iota(jnp.int32, (n_rows, PAGE_SIZE), 1)
        qi = lax.broadcasted_iota(jnp.int32, (n_rows, PAGE_SIZE), 0) % n_q
        visible = key < qi
        lk = jnp.where(visible, _neg_softplus(z), 0.0)
        incl = _dot(lk.astype(BF16), tri_ref[0:PAGE_SIZE, 0:PAGE_SIZE])
        w = jnp.where(visible, jnp.exp(z + incl), 0.0)
        acc_ref[...] = _dot(w.astype(BF16), vn)
        carry_ref[...] = jnp.broadcast_to(incl[:, 0:1], carry_ref.shape)

    @pl.when(step > 0)
    def _():
        for i in range(pps):
            lanes = slice(i * PAGE_SIZE, (i + 1) * PAGE_SIZE)
            kt_scr[:, lanes] = kt_refs[i][...].astype(BF16)
            vt_scr[:, lanes] = vt_refs[i][...].astype(BF16)
        z_all = _dot(qbd_ref[...], kt_scr[...]) + bias_ref[...]
        carry = carry_ref[:, 0:1]
        for ch in reversed(range(window // MXU_DIM)):
            lanes = slice(ch * MXU_DIM, (ch + 1) * MXU_DIM)
            z = z_all[:, lanes]
            incl = _dot(_neg_softplus(z).astype(BF16), tri_ref[...])
            w_scr[:, lanes] = jnp.exp(z + incl + carry).astype(BF16)
            carry = carry + incl[:, 0:1]
        acc_ref[...] += _dot_nt(w_scr[...], vt_scr[...])
        carry_ref[...] = jnp.broadcast_to(carry, carry_ref.shape)

    @pl.when(step == pl.num_programs(1) - 1)
    def _():
        rh = lax.broadcasted_iota(jnp.int32, (n_rows, D_MODEL), 0) // n_q
        ch = lax.broadcasted_iota(jnp.int32, (n_rows, D_MODEL), 1) // HEAD_DIM
        own = jnp.where(rh == ch, acc_ref[...], 0.0)
        out = own[0:n_q, :]
        for h in range(1, N_HEADS):
            out = out + own[h * n_q:(h + 1) * n_q, :]
        o_ref[...] = out.astype(BF16)


def _attn_sample(q, k_new, v_new, bias_rows, tri, cache_kt, cache_vt, page_table, pps):
    db, n_q, _ = q.shape
    n_pages = page_table.shape[1]
    n_rows = N_HEADS * n_q
    window = pps * PAGE_SIZE

    def page_spec(i):
        def idx(bi, si, pt):
            return (pt[bi, n_pages - pps * jnp.maximum(si, 1) + i], 0, 0)
        return pl.BlockSpec((None, D_MODEL, PAGE_SIZE), idx)

    per_seq = lambda bi, si, pt: (bi, 0, 0)
    const = lambda bi, si, pt: (0, 0)
    pages = [page_spec(i) for i in range(pps)]
    grid_spec = pltpu.PrefetchScalarGridSpec(
        num_scalar_prefetch=1,
        grid=(db, n_pages // pps + 1),
        in_specs=[pl.BlockSpec((None, n_q, D_MODEL), per_seq),
                  pl.BlockSpec((None, n_q, D_MODEL), per_seq),
                  pl.BlockSpec((None, n_q, D_MODEL), per_seq),
                  pl.BlockSpec((n_rows, 1), const),
                  pl.BlockSpec((MXU_DIM, MXU_DIM), const)] + pages + pages,
        out_specs=pl.BlockSpec((None, n_q, D_MODEL), per_seq),
        scratch_shapes=[pltpu.VMEM((n_rows, D_MODEL), BF16),
                        pltpu.VMEM((D_MODEL, window), BF16),
                        pltpu.VMEM((D_MODEL, window), BF16),
                        pltpu.VMEM((n_rows, window), BF16),
                        pltpu.VMEM((n_rows, D_MODEL), F32),
                        pltpu.VMEM((n_rows, LANES), F32)])
    return pl.pallas_call(
        functools.partial(_attn_sample_body, n_q=n_q, pps=pps),
        grid_spec=grid_spec,
        out_shape=jax.ShapeDtypeStruct((db, n_q, D_MODEL), BF16),
        compiler_params=pltpu.CompilerParams(
            dimension_semantics=("arbitrary", "arbitrary"), vmem_limit_bytes=VMEM_LIMIT),
        name="attn_sample",
    )(page_table, q, k_new, v_new, bias_rows, tri, *([cache_kt] * pps), *([cache_vt] * pps))


def _ffn_body(*refs, with_wo, short_seq, tiles_per_seq, seq_len):
    refs = list(refs)
    x_ref = refs.pop(0)
    if with_wo:
        o_ref, wo_ref = refs.pop(0), refs.pop(0)
    g_ref, wup_ref, cw_ref, cb_ref, wdn_ref = [refs.pop(0) for _ in range(5)]
    if short_seq:
        p1_ref, p2_ref = refs.pop(0), refs.pop(0)
    out_ref, conv_ref, a_scr, g_scr = refs
    tm = x_ref.shape[0]

    x1 = x_ref[...]
    if with_wo:
        x1 = x1 + _dot(o_ref[...], wo_ref[...])
    h = _rms(x1, g_ref[...]).astype(BF16)

    if short_seq:
        a_scr[0:SUBLANES, :] = jnp.zeros((SUBLANES, D_FF), F32)
    else:
        @pl.when(pl.program_id(0) % tiles_per_seq == 0)
        def _():
            a_scr[0:SUBLANES, :] = jnp.zeros((SUBLANES, D_FF), F32)

    a_scr[SUBLANES:SUBLANES + tm, :] = _dot(h, wup_ref[:, 0:D_FF])

    row = lax.broadcasted_iota(jnp.int32, (tm, FF_CHUNK), 0)
    for c in range(D_FF // FF_CHUNK):
        cs = slice(c * FF_CHUNK, (c + 1) * FF_CHUNK)
        a0 = a_scr[SUBLANES:SUBLANES + tm, cs]
        a1 = a_scr[SUBLANES - 1:SUBLANES - 1 + tm, cs]
        a2 = a_scr[SUBLANES - 2:SUBLANES - 2 + tm, cs]
        if short_seq:
            pos = row % seq_len
            a1 = jnp.where(pos < 1, p1_ref[:, cs], a1)
            a2 = jnp.where(pos < 2, p2_ref[:, cs], a2)
        b = _dot(h, wup_ref[:, D_FF + c * FF_CHUNK:D_FF + (c + 1) * FF_CHUNK])
        cc = cb_ref[:, cs] + cw_ref[0:1, cs] * a2 + cw_ref[1:2, cs] * a1 + cw_ref[2:3, cs] * a0
        g_scr[:, cs] = (cc * jax.nn.sigmoid(cc) * b).astype(BF16)

    tail = a_scr[tm:tm + SUBLANES, :]
    if short_seq:
        conv_ref[...] = a_scr[SUBLANES:SUBLANES + tm, :]
    else:
        conv_ref[...] = tail
        a_scr[0:SUBLANES, :] = tail
    out_ref[...] = x1 + _dot(g_scr[...], wdn_ref[...])


def _ffn(x2d, o2d, w_o, g_ffn, w_up, conv_w, conv_b, w_down, seq_len, tm, prev_rows=None):
    t = x2d.shape[0]
    with_wo = o2d is not None
    short_seq = prev_rows is not None
    n_tiles = t // tm
    row = lambda i: (i, 0)
    tok = pl.BlockSpec((tm, D_MODEL), row)
    args, specs = [x2d], [tok]
    if with_wo:
        args += [o2d, w_o]
        specs += [tok, _resident((D_MODEL, D_MODEL))]
    args += [g_ffn, w_up, conv_w, conv_b, w_down]
    specs += [_resident((1, D_MODEL)), _resident((D_MODEL, 2 * D_FF)), _resident((3, D_FF)),
              _resident((1, D_FF)), _resident((D_FF, D_MODEL))]
    if short_seq:
        args += list(prev_rows)
        specs += [pl.BlockSpec((tm, D_FF), row)] * 2
        conv_shape, conv_spec = (t, D_FF), pl.BlockSpec((tm, D_FF), row)
        tiles_per_seq = 1
    else:
        conv_shape = (n_tiles, SUBLANES, D_FF)
        conv_spec = pl.BlockSpec((None, SUBLANES, D_FF), lambda i: (i, 0, 0))
        tiles_per_seq = seq_len // tm
    return pl.pallas_call(
        functools.partial(_ffn_body, with_wo=with_wo, short_seq=short_seq,
                          tiles_per_seq=tiles_per_seq, seq_len=seq_len),
        grid=(n_tiles,),
        in_specs=specs,
        out_specs=[tok, conv_spec],
        out_shape=[jax.ShapeDtypeStruct((t, D_MODEL), F32),
                   jax.ShapeDtypeStruct(conv_shape, F32)],
        scratch_shapes=[pltpu.VMEM((tm + SUBLANES, D_FF), F32), pltpu.VMEM((tm, D_FF), BF16)],
        compiler_params=pltpu.CompilerParams(
            dimension_semantics=("arbitrary",), vmem_limit_bytes=VMEM_LIMIT),
        name="conv_ffn",
    )(*args)


def _s5_body(x_ref, g_ref, win_ref, bd_ref, cd_ref, are_ref, aim_ref, dskip_ref, wglu_ref, s0_ref,
             xo_ref, sfin_ref, xt_scr, st_scr, bu_scr, y_scr, *, tt):
    ti = pl.program_id(1)
    n_seq = SUBLANES

    @pl.when(ti == 0)
    def _():
        st_scr[...] = s0_ref[...]

    n_slabs = D_MODEL // LANES
    for b in range(n_seq):
        for j in range(n_slabs):
            xt_scr[j, pl.ds(b, tt, stride=n_seq), :] = x_ref[b, :, j * LANES:(j + 1) * LANES]
    xt = jnp.concatenate([xt_scr[j] for j in range(n_slabs)], axis=1)
    u = _dot(_rms(xt, g_ref[...]).astype(BF16), win_ref[...])
    ub = u.astype(BF16)

    for k in range(N_SSM_BLOCKS):
        bu_scr[...] = _dot(ub[:, k * SSM_BLOCK_IN:(k + 1) * SSM_BLOCK_IN], bd_ref[k])
        for lc in range(SSM_BLOCK_STATE // SSM_SCAN_LANES):
            re_l = slice(lc * SSM_SCAN_LANES, (lc + 1) * SSM_SCAN_LANES)
            im_l = slice(SSM_BLOCK_STATE + lc * SSM_SCAN_LANES,
                         SSM_BLOCK_STATE + (lc + 1) * SSM_SCAN_LANES)
            base = k * 2 * SSM_BLOCK_STATE
            st_re = slice(base + re_l.start, base + re_l.stop)
            st_im = slice(base + im_l.start, base + im_l.stop)
            ar = jnp.broadcast_to(are_ref[k, :, re_l], (n_seq, SSM_SCAN_LANES))
            ai = jnp.broadcast_to(aim_ref[k, :, re_l], (n_seq, SSM_SCAN_LANES))

            def step(t, s, ar=ar, ai=ai, re_l=re_l, im_l=im_l):
                sr, si = s
                r0 = pl.multiple_of(t * n_seq, n_seq)
                nr = ar * sr - ai * si + bu_scr[pl.ds(r0, n_seq), re_l]
                ni = ar * si + ai * sr + bu_scr[pl.ds(r0, n_seq), im_l]
                bu_scr[pl.ds(r0, n_seq), re_l] = nr
                bu_scr[pl.ds(r0, n_seq), im_l] = ni
                return nr, ni

            sr, si = lax.fori_loop(0, tt, step, (st_scr[:, st_re], st_scr[:, st_im]), unroll=4)
            st_scr[:, st_re] = sr
            st_scr[:, st_im] = si
        y_scr[:, k * SSM_BLOCK_IN:(k + 1) * SSM_BLOCK_IN] = _dot(bu_scr[...].astype(BF16), cd_ref[k])

    y = y_scr[...] + dskip_ref[...] * u
    gg = _dot(jax.nn.gelu(y).astype(BF16), wglu_ref[...])
    x_new = xt + gg[:, 0:D_MODEL] * jax.nn.sigmoid(gg[:, D_MODEL:2 * D_MODEL])
    for j in range(n_slabs):
        xt_scr[j] = x_new[:, j * LANES:(j + 1) * LANES]
    for b in range(n_seq):
        for j in range(n_slabs):
            xo_ref[b, :, j * LANES:(j + 1) * LANES] = xt_scr[j, pl.ds(b, tt, stride=n_seq), :]

    @pl.when(ti == pl.num_programs(1) - 1)
    def _():
        sfin_ref[...] = st_scr[...]


def _s5(x, g_mix, w_in, bd, cd, a_re, a_im, d_skip, w_glu, s0, tt):
    b, s, _ = x.shape
    n_state = N_SSM_BLOCKS * 2 * SSM_BLOCK_STATE
    x_spec = pl.BlockSpec((SUBLANES, tt, D_MODEL), lambda bi, ti: (bi, ti, 0))
    s_spec = pl.BlockSpec((None, SUBLANES, n_state), lambda bi, ti: (bi, 0, 0))
    rows = tt * SUBLANES
    return pl.pallas_call(
        functools.partial(_s5_body, tt=tt),
        grid=(b // SUBLANES, s // tt),
        in_specs=[x_spec, _resident((1, D_MODEL)), _resident((D_MODEL, D_MODEL)),
                  _resident(bd.shape), _resident(cd.shape), _resident(a_re.shape),
                  _resident(a_im.shape), _resident((1, D_MODEL)),
                  _resident((D_MODEL, 2 * D_MODEL)), s_spec],
        out_specs=[x_spec, s_spec],
        out_shape=[jax.ShapeDtypeStruct((b, s, D_MODEL), F32),
                   jax.ShapeDtypeStruct((b // SUBLANES, SUBLANES, n_state), F32)],
        scratch_shapes=[pltpu.VMEM((D_MODEL // LANES, rows, LANES), F32),
                        pltpu.VMEM((SUBLANES, n_state), F32),
                        pltpu.VMEM((rows, 2 * SSM_BLOCK_STATE), F32),
                        pltpu.VMEM((rows, D_MODEL), F32)],
        compiler_params=pltpu.CompilerParams(
            dimension_semantics=("arbitrary", "arbitrary"), vmem_limit_bytes=VMEM_LIMIT),
        name="s5_layer",
    )(x, g_mix, w_in, bd, cd, a_re, a_im, d_skip, w_glu, s0)


def _s5_tables(lam_re, lam_im, log_dt, b_re, b_im, c_re, c_im):
    dt = jnp.exp(log_dt)[:, None]
    mag = jnp.exp(lam_re * dt)
    a_re = mag * jnp.cos(lam_im * dt)
    a_im = mag * jnp.sin(lam_im * dt)
    den = lam_re * lam_re + lam_im * lam_im
    zr = ((a_re - 1.0) * lam_re + a_im * lam_im) / den
    zi = (a_im * lam_re - (a_re - 1.0) * lam_im) / den
    bbar_re = zr[..., None] * b_re - zi[..., None] * b_im
    bbar_im = zr[..., None] * b_im + zi[..., None] * b_re
    eye = jnp.eye(SSM_GROUPS_PER_BLOCK, dtype=F32)

    def pack_in(m):
        m = m.reshape(N_SSM_BLOCKS, SSM_GROUPS_PER_BLOCK, STATE_DIM, SSM_GROUP)
        full = jnp.einsum("kgpc,gh->kgchp", m, eye)
        return full.reshape(N_SSM_BLOCKS, SSM_BLOCK_IN, SSM_BLOCK_STATE)

    def pack_out(m):
        m = m.reshape(N_SSM_BLOCKS, SSM_GROUPS_PER_BLOCK, SSM_GROUP, STATE_DIM)
        full = jnp.einsum("kgcp,gh->kgphc", m, eye)
        return full.reshape(N_SSM_BLOCKS, SSM_BLOCK_STATE, SSM_BLOCK_IN)

    bd = jnp.concatenate([pack_in(bbar_re), pack_in(bbar_im)], axis=2).astype(BF16)
    cd = jnp.concatenate([pack_out(c_re), pack_out(-c_im)], axis=1).astype(BF16)
    shape = (N_SSM_BLOCKS, 1, SSM_BLOCK_STATE)
    return bd, cd, a_re.reshape(shape), a_im.reshape(shape)


def _pack_state(s_re, s_im):
    n = s_re.shape[0]
    packed = jnp.concatenate([s_re.reshape(n, N_SSM_BLOCKS, SSM_BLOCK_STATE),
                              s_im.reshape(n, N_SSM_BLOCKS, SSM_BLOCK_STATE)], axis=2)
    return packed.reshape(n // SUBLANES, SUBLANES, -1)


def _unpack_state(packed, n):
    s = packed.reshape(n, N_SSM_BLOCKS, 2, SSM_BLOCK_STATE)
    return (s[:, :, 0].reshape(n, N_GROUPS, STATE_DIM), s[:, :, 1].reshape(n, N_GROUPS, STATE_DIM))


def _trunk(x, w, tables, *, tm, tq, tt, past=None, ssm_prev=None, conv_prev=None):
    b, s, _ = x.shape
    t = b * s
    x2d = x.reshape(t, D_MODEL)

    shape3 = (b, s, D_MODEL)
    heads = (b, s, N_HEADS, HEAD_DIM)
    if past is None:
        q, kt, vt, ktb, vtb = _qkv_t(x2d, w["norm_mix"][0], w["wq"], w["wkt"], w["wvt"], w["q_gain"],
                                     w["k_gain_col"], s, tm)
        o = _attn_prompt(q.reshape(shape3), ktb, vtb, w["bias"], w["tri"], tq, 2)
        feat_major = lambda a: jnp.transpose(a.reshape(b, N_HEADS, HEAD_DIM, s), (0, 3, 1, 2))
        k, v = feat_major(kt), feat_major(vt)
    else:
        q, k, v = _qkv(x2d, w["norm_mix"][0], w["wq"], w["wk"], w["wv"], w["q_gain"], w["k_gain"], tm)
        cache_kt, cache_vt, page_table, pps = past
        o = _attn_sample(q.reshape(shape3), k.reshape(shape3), v.reshape(shape3), w["bias_rows"],
                         w["tri"], cache_kt, cache_vt, page_table, pps)
        k, v = k.reshape(heads), v.reshape(heads)
    prev0 = None if conv_prev is None else conv_prev[0]
    x2d, conv0 = _ffn(x2d, o.reshape(t, D_MODEL), w["w_o"], w["norm_ffn"][0], w["w_up"][0],
                      w["conv_w"][0], w["conv_b"][0], w["w_down"][0], s, tm, prev0)

    bd, cd, a_re, a_im = tables
    if ssm_prev is None:
        s0 = jnp.zeros((b // SUBLANES, SUBLANES, N_SSM_BLOCKS * 2 * SSM_BLOCK_STATE), F32)
    else:
        s0 = _pack_state(*ssm_prev)
    x3, s_fin = _s5(x2d.reshape(shape3), w["norm_mix"][1], w["w_in"], bd, cd, a_re, a_im,
                    w["d_skip"], w["w_glu"], s0, tt)
    s_re, s_im = _unpack_state(s_fin, b)
    prev1 = None if conv_prev is None else conv_prev[1]
    y2d, conv1 = _ffn(x3.reshape(t, D_MODEL), None, None, w["norm_ffn"][1], w["w_up"][1],
                      w["conv_w"][1], w["conv_b"][1], w["w_down"][1], s, tm, prev1)

    def conv_state(c):
        if conv_prev is None:
            per_seq = c.reshape(b, s // tm, SUBLANES, D_FF)[:, -1]
        else:
            per_seq = c.reshape(b, s, D_FF)
        return per_seq[:, -2:]

    return (y2d.reshape(shape3), k[None], v[None], s_re[None], s_im[None],
            jnp.stack([conv_state(conv0), conv_state(conv1)]))


def _short_seq_prev_rows(prev, seq_len):
    bsz = prev.shape[0]
    zeros = jnp.zeros((bsz, seq_len - 2, D_FF), F32)
    p2 = jnp.concatenate([prev, zeros], axis=1)
    p1 = jnp.concatenate([prev[:, 1:2], jnp.zeros((bsz, seq_len - 1, D_FF), F32)], axis=1)
    return p1.reshape(bsz * seq_len, D_FF), p2.reshape(bsz * seq_len, D_FF)


def kernel(x_prompt, x_sample, cache_k, cache_v, state_ssm_re, state_ssm_im, state_ffn_conv, page_table,
           norm_mix, norm_ffn, attn_w_qkv, attn_q_gain, attn_k_gain, attn_logit_bias, attn_w_o,
           ssm_w_in, ssm_lambda_re, ssm_lambda_im, ssm_log_dt, ssm_b_re, ssm_b_im, ssm_c_re, ssm_c_im,
           ssm_d, ssm_w_glu, ffn_w_up, ffn_conv_w, ffn_conv_b, ffn_w_down):
    dec_batch, dec_seq, _ = x_sample.shape
    w = {
        "norm_mix": norm_mix[:, None, :], "norm_ffn": norm_ffn[:, None, :],
        "wq": attn_w_qkv[0, :, 0:D_MODEL].astype(BF16),
        "wk": attn_w_qkv[0, :, D_MODEL:2 * D_MODEL].astype(BF16),
        "wv": attn_w_qkv[0, :, 2 * D_MODEL:3 * D_MODEL].astype(BF16),
        "wkt": attn_w_qkv[0, :, D_MODEL:2 * D_MODEL].T.astype(BF16),
        "wvt": attn_w_qkv[0, :, 2 * D_MODEL:3 * D_MODEL].T.astype(BF16),
        "q_gain": jnp.tile(attn_q_gain[0], N_HEADS)[None], "k_gain": jnp.tile(attn_k_gain[0], N_HEADS)[None],
        "k_gain_col": attn_k_gain[0][:, None],
        "bias": attn_logit_bias[0],
        "bias_rows": jnp.repeat(attn_logit_bias[0], dec_seq)[:, None],
        "tri": jnp.tril(jnp.ones((MXU_DIM, MXU_DIM), BF16)),
        "w_o": attn_w_o[0].astype(BF16),
        "w_in": ssm_w_in[0].astype(BF16), "d_skip": ssm_d[0][None], "w_glu": ssm_w_glu[0].astype(BF16),
        "w_up": ffn_w_up.astype(BF16), "conv_w": ffn_conv_w, "conv_b": ffn_conv_b[:, None, :],
        "w_down": ffn_w_down.astype(BF16),
    }
    tables = _s5_tables(ssm_lambda_re[0], ssm_lambda_im[0], ssm_log_dt[0], ssm_b_re[0], ssm_b_im[0],
                        ssm_c_re[0], ssm_c_im[0])

    y_p, k_p, v_p, sr_p, si_p, conv_p = _trunk(x_prompt, w, tables, tm=512, tq=256, tt=64)

    n_pool = cache_k.shape[1]
    to_pages = lambda c: jnp.transpose(c[0], (0, 2, 3, 1)).reshape(n_pool, D_MODEL, PAGE_SIZE)
    past = (to_pages(cache_k), to_pages(cache_v), page_table, 8)
    conv_prev = [_short_seq_prev_rows(state_ffn_conv[i], dec_seq) for i in range(state_ffn_conv.shape[0])]
    y_s, k_s, v_s, sr_s, si_s, conv_s = _trunk(
        x_sample, w, tables, tm=dec_batch * dec_seq, tq=None, tt=dec_seq,
        past=past, ssm_prev=(state_ssm_re[0], state_ssm_im[0]), conv_prev=conv_prev)

    return (y_p, y_s, k_p, v_p, k_s, v_s, sr_p, si_p, sr_s, si_s, conv_p, conv_s)
```

```python
import functools
import math

import jax
import jax.numpy as jnp
from jax import lax
from jax.experimental import pallas as pl
from jax.experimental.pallas import tpu as pltpu

F32 = jnp.float32
BF16 = jnp.bfloat16

D_MODEL = 1024
N_HEADS = 16
HEAD_DIM = 64
PAGE_SIZE = 128
SSM_GROUP = 16
N_GROUPS = D_MODEL // SSM_GROUP
STATE_DIM = 64
D_FF = 2816
NORM_EPS = 1e-6

LANES = 128
SUBLANES = 8
MXU_DIM = 256
VMEM_LIMIT = 56 * 1024 * 1024

HEADS_PER_SLAB = LANES // HEAD_DIM
SSM_BLOCK_IN = MXU_DIM
SSM_GROUPS_PER_BLOCK = SSM_BLOCK_IN // SSM_GROUP
SSM_BLOCK_STATE = SSM_GROUPS_PER_BLOCK * STATE_DIM
N_SSM_BLOCKS = D_MODEL // SSM_BLOCK_IN
SSM_SCAN_LANES = 512
FF_CHUNK = MXU_DIM
LOG2_E = math.log2(math.e)
MASKED_LOGIT = -1e30


def _dot(a, b):
    return jnp.dot(a, b, preferred_element_type=F32)


def _dot_nt(a, b):
    return lax.dot_general(a, b, (((1,), (1,)), ((), ())), preferred_element_type=F32)


def _rms(x, gain):
    return x * lax.rsqrt(jnp.mean(x * x, axis=-1, keepdims=True) + NORM_EPS) * gain


def _resident(shape):
    nd = len(shape)
    return pl.BlockSpec(shape, lambda *_: (0,) * nd, pipeline_mode=pl.Buffered(1))


def _neg_softplus(z):
    return -(jnp.maximum(z, 0.0) + jnp.log(1.0 + jnp.exp(-jnp.abs(z))))


def _head_norm_rows(t, gain):
    r = lax.broadcasted_iota(jnp.int32, (MXU_DIM, MXU_DIM), 0) // HEAD_DIM
    c = lax.broadcasted_iota(jnp.int32, (MXU_DIM, MXU_DIM), 1) // HEAD_DIM
    seg = (r == c).astype(BF16)
    sq = (t * t).astype(BF16)
    ms = jnp.concatenate(
        [_dot(sq[:, j * MXU_DIM:(j + 1) * MXU_DIM], seg) for j in range(D_MODEL // MXU_DIM)],
        axis=1) * (1.0 / HEAD_DIM)
    return t * lax.rsqrt(ms + NORM_EPS) * gain


def _qkv_body(x_ref, gm_ref, wq_ref, wk_ref, wv_ref, qg_ref, kg_ref, q_ref, k_ref, v_ref):
    h = _rms(x_ref[...], gm_ref[...]).astype(BF16)
    q = _head_norm_rows(_dot(h, wq_ref[...]), qg_ref[...])
    q_ref[...] = (q * (HEAD_DIM ** -0.5)).astype(BF16)
    k_ref[...] = _head_norm_rows(_dot(h, wk_ref[...]), kg_ref[...])
    v_ref[...] = _dot(h, wv_ref[...])


def _qkv(x2d, g_mix, wq, wk, wv, q_gain, k_gain, tm):
    t = x2d.shape[0]
    tok = pl.BlockSpec((tm, D_MODEL), lambda i: (i, 0))
    wspec = _resident((D_MODEL, D_MODEL))
    return pl.pallas_call(
        _qkv_body,
        grid=(t // tm,),
        in_specs=[tok, _resident((1, D_MODEL)), wspec, wspec, wspec,
                  _resident((1, D_MODEL)), _resident((1, D_MODEL))],
        out_specs=[tok, tok, tok],
        out_shape=[jax.ShapeDtypeStruct((t, D_MODEL), BF16),
                   jax.ShapeDtypeStruct((t, D_MODEL), F32),
                   jax.ShapeDtypeStruct((t, D_MODEL), F32)],
        compiler_params=pltpu.CompilerParams(
            dimension_semantics=("arbitrary",), vmem_limit_bytes=VMEM_LIMIT),
        name="qkv_proj",
    )(x2d, g_mix, wq, wk, wv, q_gain, k_gain)


def _qkv_t_body(x_ref, gm_ref, wq_ref, wkt_ref, wvt_ref, qg_ref, kgt_ref,
                q_ref, kt_ref, vt_ref, ktb_ref, vtb_ref):
    tm = x_ref.shape[0]
    h = _rms(x_ref[...], gm_ref[...]).astype(BF16)
    q = _head_norm_rows(_dot(h, wq_ref[...]), qg_ref[...])
    q_ref[...] = (q * (HEAD_DIM ** -0.5 * LOG2_E)).astype(BF16)
    k3 = _dot_nt(wkt_ref[...], h).reshape(N_HEADS, HEAD_DIM, tm)
    ms = jnp.mean(k3 * k3, axis=1, keepdims=True)
    kt = (k3 * lax.rsqrt(ms + NORM_EPS) * kgt_ref[...][None]).reshape(D_MODEL, tm)
    kt_ref[...] = kt
    ktb_ref[...] = kt.astype(BF16)
    vt = _dot_nt(wvt_ref[...], h)
    vt_ref[...] = vt
    vtb_ref[...] = vt.astype(BF16)


def _qkv_t(x2d, g_mix, wq, wkt, wvt, q_gain, k_gain_col, seq_len, tm):
    t = x2d.shape[0]
    tiles_per_seq = seq_len // tm
    tok = pl.BlockSpec((tm, D_MODEL), lambda i: (i, 0))
    feat = pl.BlockSpec((None, D_MODEL, tm), lambda i: (i // tiles_per_seq, 0, i % tiles_per_seq))
    wspec = _resident((D_MODEL, D_MODEL))
    shape_t = (t // seq_len, D_MODEL, seq_len)
    return pl.pallas_call(
        _qkv_t_body,
        grid=(t // tm,),
        in_specs=[tok, _resident((1, D_MODEL)), wspec, wspec, wspec,
                  _resident((1, D_MODEL)), _resident((HEAD_DIM, 1))],
        out_specs=[tok, feat, feat, feat, feat],
        out_shape=[jax.ShapeDtypeStruct((t, D_MODEL), BF16),
                   jax.ShapeDtypeStruct(shape_t, F32),
                   jax.ShapeDtypeStruct(shape_t, F32),
                   jax.ShapeDtypeStruct(shape_t, BF16),
                   jax.ShapeDtypeStruct(shape_t, BF16)],
        compiler_params=pltpu.CompilerParams(
            dimension_semantics=("arbitrary",), vmem_limit_bytes=VMEM_LIMIT),
        name="qkv_proj_t",
    )(x2d, g_mix, wq, wkt, wvt, q_gain, k_gain_col)


def _attn_prompt_body(bias_ref, tri_ref, q_ref, kt_ref, vt_ref, o_ref, acc_ref, z_scr, i_scr,
                      *, tq, n_slabs):
    grp = pl.program_id(1)
    i = pl.program_id(2)
    n_heads = n_slabs * HEADS_PER_SLAB
    low = lax.broadcasted_iota(jnp.int32, (tq, LANES), 1) < HEAD_DIM
    q_heads = []
    for s in range(n_slabs):
        q = q_ref[:, s * LANES:(s + 1) * LANES]
        q_heads += [jnp.where(low, q, jnp.zeros_like(q)), jnp.where(low, jnp.zeros_like(q), q)]
    bias = [bias_ref[grp * n_heads + h] for h in range(n_heads)]
    row = lax.broadcasted_iota(jnp.int32, (tq, tq), 0)
    col = lax.broadcasted_iota(jnp.int32, (tq, tq), 1)
    causal = col < row

    acc_ref[...] = jnp.zeros_like(acc_ref)

    def rows_of(h):
        return slice((h // HEADS_PER_SLAB) * LANES, (h // HEADS_PER_SLAB + 1) * LANES)

    def stage_a(j, diag):
        keys = pl.ds(pl.multiple_of(j * tq, tq), tq)
        totals = []
        for h in range(n_heads):
            z = _dot(q_heads[h], kt_ref[rows_of(h), keys]) + bias[h]
            if diag:
                z = jnp.where(causal, z, MASKED_LOGIT)
            sp = jnp.maximum(z, 0.0) + jnp.log2(1.0 + jnp.exp2(-jnp.abs(z)))
            incl = _dot(sp.astype(BF16), tri_ref[...])
            z_scr[h] = z
            i_scr[h] = incl
            totals.append(incl[:, 0:1])
        return tuple(totals)

    def stage_b(j, carry):
        keys = pl.ds(pl.multiple_of(j * tq, tq), tq)
        for h in range(n_heads):
            w = jnp.exp2(z_scr[h] + i_scr[h] + carry[h])
            acc_ref[h] += _dot_nt(w.astype(BF16), vt_ref[rows_of(h), keys])

    def step(jj, state):
        carry, totals = state
        j = i - 1 - jj
        stage_b(j + 1, carry)
        carry = tuple(c + t for c, t in zip(carry, totals))
        return carry, stage_a(j, False)

    zero = jnp.zeros((tq, 1), F32)
    carry, _ = lax.fori_loop(0, i, step, ((zero,) * n_heads, stage_a(i, True)))
    stage_b(0, carry)
    for s in range(n_slabs):
        o_ref[:, s * LANES:(s + 1) * LANES] = jnp.where(
            low, acc_ref[HEADS_PER_SLAB * s], acc_ref[HEADS_PER_SLAB * s + 1]).astype(BF16)


def _attn_prompt(q, kt, vt, bias, tri, tq, n_slabs):
    b, s, _ = q.shape
    width = n_slabs * LANES
    kv_spec = pl.BlockSpec((None, width, s), lambda bi, gi, qi: (bi, gi, 0))
    q_spec = pl.BlockSpec((None, tq, width), lambda bi, gi, qi: (bi, qi, gi))
    return pl.pallas_call(
        functools.partial(_attn_prompt_body, tq=tq, n_slabs=n_slabs),
        grid=(b, D_MODEL // width, s // tq),
        in_specs=[pl.BlockSpec(memory_space=pltpu.SMEM),
                  pl.BlockSpec((tq, tq), lambda bi, gi, qi: (0, 0)), q_spec, kv_spec, kv_spec],
        out_specs=q_spec,
        out_shape=jax.ShapeDtypeStruct((b, s, D_MODEL), BF16),
        scratch_shapes=[pltpu.VMEM((n_slabs * HEADS_PER_SLAB, tq, LANES), F32),
                        pltpu.VMEM((n_slabs * HEADS_PER_SLAB, tq, tq), F32),
                        pltpu.VMEM((n_slabs * HEADS_PER_SLAB, tq, tq), F32)],
        compiler_params=pltpu.CompilerParams(
            dimension_semantics=("arbitrary", "arbitrary", "arbitrary"),
            vmem_limit_bytes=VMEM_LIMIT),
        name="attn_prompt",
    )(bias, tri, q, kt, vt)


def _attn_sample_body(pt_ref, q_ref, kn_ref, vn_ref, bias_ref, tri_ref, *rest, n_q, pps):
    del pt_ref
    kt_refs, vt_refs = rest[:pps], rest[pps:2 * pps]
    o_ref, qbd_ref, kt_scr, vt_scr, w_scr, acc_ref, carry_ref = rest[2 * pps:]
    step = pl.program_id(1)
    n_rows = N_HEADS * n_q
    window = pps * PAGE_SIZE

    @pl.when(step == 0)
    def _():
        q_rep = jnp.concatenate([q_ref[...].astype(F32)] * N_HEADS, axis=0)
        rh = lax.broadcasted_iota(jnp.int32, (n_rows, D_MODEL), 0) // n_q
        ch = lax.broadcasted_iota(jnp.int32, (n_rows, D_MODEL), 1) // HEAD_DIM
        qbd = jnp.where(rh == ch, q_rep, 0.0).astype(BF16)
        qbd_ref[...] = qbd
        pad = jnp.zeros((PAGE_SIZE - n_q, D_MODEL), F32)
        kn = jnp.concatenate([kn_ref[...], pad], axis=0).astype(BF16)
        vn = jnp.concatenate([vn_ref[...], pad], axis=0).astype(BF16)
        z = _dot_nt(qbd, kn) + bias_ref[...]
        key = lax.broadcasted_iota(jnp.int32, (n_rows, PAGE_SIZE), 1)
        qi = lax.broadcasted_iota(jnp.int32, (n_rows, PAGE_SIZE), 0) % n_q
        visible = key < qi
        lk = jnp.where(visible, _neg_softplus(z), 0.0)
        incl = _dot(lk.astype(BF16), tri_ref[0:PAGE_SIZE, 0:PAGE_SIZE])
        w = jnp.where(visible, jnp.exp(z + incl), 0.0)
        acc_ref[...] = _dot(w.astype(BF16), vn)
        carry_ref[...] = jnp.broadcast_to(incl[:, 0:1], carry_ref.shape)

    @pl.when(step > 0)
    def _():
        for i in range(pps):
            lanes = slice(i * PAGE_SIZE, (i + 1) * PAGE_SIZE)
            kt_scr[:, lanes] = kt_refs[i][...].astype(BF16)
            vt_scr[:, lanes] = vt_refs[i][...].astype(BF16)
        z_all = _dot(qbd_ref[...], kt_scr[...]) + bias_ref[...]
        carry = carry_ref[:, 0:1]
        for ch in reversed(range(window // MXU_DIM)):
            lanes = slice(ch * MXU_DIM, (ch + 1) * MXU_DIM)
            z = z_all[:, lanes]
            incl = _dot(_neg_softplus(z).astype(BF16), tri_ref[...])
            w_scr[:, lanes] = jnp.exp(z + incl + carry).astype(BF16)
            carry = carry + incl[:, 0:1]
        acc_ref[...] += _dot_nt(w_scr[...], vt_scr[...])
        carry_ref[...] = jnp.broadcast_to(carry, carry_ref.shape)

    @pl.when(step == pl.num_programs(1) - 1)
    def _():
        rh = lax.broadcasted_iota(jnp.int32, (n_rows, D_MODEL), 0) // n_q
        ch = lax.broadcasted_iota(jnp.int32, (n_rows, D_MODEL), 1) // HEAD_DIM
        own = jnp.where(rh == ch, acc_ref[...], 0.0)
        out = own[0:n_q, :]
        for h in range(1, N_HEADS):
            out = out + own[h * n_q:(h + 1) * n_q, :]
        o_ref[...] = out.astype(BF16)


def _attn_sample(q, k_new, v_new, bias_rows, tri, cache_kt, cache_vt, page_table, pps):
    db, n_q, _ = q.shape
    n_pages = page_table.shape[1]
    n_rows = N_HEADS * n_q
    window = pps * PAGE_SIZE

    def page_spec(i):
        def idx(bi, si, pt):
            return (pt[bi, n_pages - pps * jnp.maximum(si, 1) + i], 0, 0)
        return pl.BlockSpec((None, D_MODEL, PAGE_SIZE), idx)

    per_seq = lambda bi, si, pt: (bi, 0, 0)
    const = lambda bi, si, pt: (0, 0)
    pages = [page_spec(i) for i in range(pps)]
    grid_spec = pltpu.PrefetchScalarGridSpec(
        num_scalar_prefetch=1,
        grid=(db, n_pages // pps + 1),
        in_specs=[pl.BlockSpec((None, n_q, D_MODEL), per_seq),
                  pl.BlockSpec((None, n_q, D_MODEL), per_seq),
                  pl.BlockSpec((None, n_q, D_MODEL), per_seq),
                  pl.BlockSpec((n_rows, 1), const),
                  pl.BlockSpec((MXU_DIM, MXU_DIM), const)] + pages + pages,
        out_specs=pl.BlockSpec((None, n_q, D_MODEL), per_seq),
        scratch_shapes=[pltpu.VMEM((n_rows, D_MODEL), BF16),
                        pltpu.VMEM((D_MODEL, window), BF16),
                        pltpu.VMEM((D_MODEL, window), BF16),
                        pltpu.VMEM((n_rows, window), BF16),
                        pltpu.VMEM((n_rows, D_MODEL), F32),
                        pltpu.VMEM((n_rows, LANES), F32)])
    return pl.pallas_call(
        functools.partial(_attn_sample_body, n_q=n_q, pps=pps),
        grid_spec=grid_spec,
        out_shape=jax.ShapeDtypeStruct((db, n_q, D_MODEL), BF16),
        compiler_params=pltpu.CompilerParams(
            dimension_semantics=("arbitrary", "arbitrary"), vmem_limit_bytes=VMEM_LIMIT),
        name="attn_sample",
    )(page_table, q, k_new, v_new, bias_rows, tri, *([cache_kt] * pps), *([cache_vt] * pps))


def _ffn_body(*refs, with_wo, short_seq, tiles_per_seq, seq_len):
    refs = list(refs)
    x_ref = refs.pop(0)
    if with_wo:
        o_ref, wo_ref = refs.pop(0), refs.pop(0)
    g_ref, wup_ref, cw_ref, cb_ref, wdn_ref = [refs.pop(0) for _ in range(5)]
    if short_seq:
        p1_ref, p2_ref = refs.pop(0), refs.pop(0)
    out_ref, conv_ref, a_scr, g_scr = refs
    tm = x_ref.shape[0]

    x1 = x_ref[...]
    if with_wo:
        x1 = x1 + _dot(o_ref[...], wo_ref[...])
    h = _rms(x1, g_ref[...]).astype(BF16)

    if short_seq:
        a_scr[0:SUBLANES, :] = jnp.zeros((SUBLANES, D_FF), F32)
    else:
        @pl.when(pl.program_id(0) % tiles_per_seq == 0)
        def _():
            a_scr[0:SUBLANES, :] = jnp.zeros((SUBLANES, D_FF), F32)

    a_scr[SUBLANES:SUBLANES + tm, :] = _dot(h, wup_ref[:, 0:D_FF])

    row = lax.broadcasted_iota(jnp.int32, (tm, FF_CHUNK), 0)
    for c in range(D_FF // FF_CHUNK):
        cs = slice(c * FF_CHUNK, (c + 1) * FF_CHUNK)
        a0 = a_scr[SUBLANES:SUBLANES + tm, cs]
        a1 = a_scr[SUBLANES - 1:SUBLANES - 1 + tm, cs]
        a2 = a_scr[SUBLANES - 2:SUBLANES - 2 + tm, cs]
        if short_seq:
            pos = row % seq_len
            a1 = jnp.where(pos < 1, p1_ref[:, cs], a1)
            a2 = jnp.where(pos < 2, p2_ref[:, cs], a2)
        b = _dot(h, wup_ref[:, D_FF + c * FF_CHUNK:D_FF + (c + 1) * FF_CHUNK])
        cc = cb_ref[:, cs] + cw_ref[0:1, cs] * a2 + cw_ref[1:2, cs] * a1 + cw_ref[2:3, cs] * a0
        g_scr[:, cs] = (cc * jax.nn.sigmoid(cc) * b).astype(BF16)

    tail = a_scr[tm:tm + SUBLANES, :]
    if short_seq:
        conv_ref[...] = a_scr[SUBLANES:SUBLANES + tm, :]
    else:
        conv_ref[...] = tail
        a_scr[0:SUBLANES, :] = tail
    out_ref[...] = x1 + _dot(g_scr[...], wdn_ref[...])


def _ffn(x2d, o2d, w_o, g_ffn, w_up, conv_w, conv_b, w_down, seq_len, tm, prev_rows=None):
    t = x2d.shape[0]
    with_wo = o2d is not None
    short_seq = prev_rows is not None
    n_tiles = t // tm
    row = lambda i: (i, 0)
    tok = pl.BlockSpec((tm, D_MODEL), row)
    args, specs = [x2d], [tok]
    if with_wo:
        args += [o2d, w_o]
        specs += [tok, _resident((D_MODEL, D_MODEL))]
    args += [g_ffn, w_up, conv_w, conv_b, w_down]
    specs += [_resident((1, D_MODEL)), _resident((D_MODEL, 2 * D_FF)), _resident((3, D_FF)),
              _resident((1, D_FF)), _resident((D_FF, D_MODEL))]
    if short_seq:
        args += list(prev_rows)
        specs += [pl.BlockSpec((tm, D_FF), row)] * 2
        conv_shape, conv_spec = (t, D_FF), pl.BlockSpec((tm, D_FF), row)
        tiles_per_seq = 1
    else:
        conv_shape = (n_tiles, SUBLANES, D_FF)
        conv_spec = pl.BlockSpec((None, SUBLANES, D_FF), lambda i: (i, 0, 0))
        tiles_per_seq = seq_len // tm
    return pl.pallas_call(
        functools.partial(_ffn_body, with_wo=with_wo, short_seq=short_seq,
                          tiles_per_seq=tiles_per_seq, seq_len=seq_len),
        grid=(n_tiles,),
        in_specs=specs,
        out_specs=[tok, conv_spec],
        out_shape=[jax.ShapeDtypeStruct((t, D_MODEL), F32),
                   jax.ShapeDtypeStruct(conv_shape, F32)],
        scratch_shapes=[pltpu.VMEM((tm + SUBLANES, D_FF), F32), pltpu.VMEM((tm, D_FF), BF16)],
        compiler_params=pltpu.CompilerParams(
            dimension_semantics=("arbitrary",), vmem_limit_bytes=VMEM_LIMIT),
        name="conv_ffn",
    )(*args)


def _s5_body(x_ref, g_ref, win_ref, bd_ref, cd_ref, are_ref, aim_ref, dskip_ref, wglu_ref, s0_ref,
             xo_ref, sfin_ref, xt_scr, st_scr, bu_scr, y_scr, *, tt):
    ti = pl.program_id(1)
    n_seq = SUBLANES

    @pl.when(ti == 0)
    def _():
        st_scr[...] = s0_ref[...]

    n_slabs = D_MODEL // LANES
    for b in range(n_seq):
        for j in range(n_slabs):
            xt_scr[j, pl.ds(b, tt, stride=n_seq), :] = x_ref[b, :, j * LANES:(j + 1) * LANES]
    xt = jnp.concatenate([xt_scr[j] for j in range(n_slabs)], axis=1)
    u = _dot(_rms(xt, g_ref[...]).astype(BF16), win_ref[...])
    ub = u.astype(BF16)

    for k in range(N_SSM_BLOCKS):
        bu_scr[...] = _dot(ub[:, k * SSM_BLOCK_IN:(k + 1) * SSM_BLOCK_IN], bd_ref[k])
        for lc in range(SSM_BLOCK_STATE // SSM_SCAN_LANES):
            re_l = slice(lc * SSM_SCAN_LANES, (lc + 1) * SSM_SCAN_LANES)
            im_l = slice(SSM_BLOCK_STATE + lc * SSM_SCAN_LANES,
                         SSM_BLOCK_STATE + (lc + 1) * SSM_SCAN_LANES)
            base = k * 2 * SSM_BLOCK_STATE
            st_re = slice(base + re_l.start, base + re_l.stop)
            st_im = slice(base + im_l.start, base + im_l.stop)
            ar = jnp.broadcast_to(are_ref[k, :, re_l], (n_seq, SSM_SCAN_LANES))
            ai = jnp.broadcast_to(aim_ref[k, :, re_l], (n_seq, SSM_SCAN_LANES))

            def step(t, s, ar=ar, ai=ai, re_l=re_l, im_l=im_l):
                sr, si = s
                r0 = pl.multiple_of(t * n_seq, n_seq)
                nr = ar * sr - ai * si + bu_scr[pl.ds(r0, n_seq), re_l]
                ni = ar * si + ai * sr + bu_scr[pl.ds(r0, n_seq), im_l]
                bu_scr[pl.ds(r0, n_seq), re_l] = nr
                bu_scr[pl.ds(r0, n_seq), im_l] = ni
                return nr, ni

            sr, si = lax.fori_loop(0, tt, step, (st_scr[:, st_re], st_scr[:, st_im]), unroll=True)
            st_scr[:, st_re] = sr
            st_scr[:, st_im] = si
        y_scr[:, k * SSM_BLOCK_IN:(k + 1) * SSM_BLOCK_IN] = _dot(bu_scr[...].astype(BF16), cd_ref[k])

    y = y_scr[...] + dskip_ref[...] * u
    gg = _dot(jax.nn.gelu(y).astype(BF16), wglu_ref[...])
    x_new = xt + gg[:, 0:D_MODEL] * jax.nn.sigmoid(gg[:, D_MODEL:2 * D_MODEL])
    for j in range(n_slabs):
        xt_scr[j] = x_new[:, j * LANES:(j + 1) * LANES]
    for b in range(n_seq):
        for j in range(n_slabs):
            xo_ref[b, :, j * LANES:(j + 1) * LANES] = xt_scr[j, pl.ds(b, tt, stride=n_seq), :]

    @pl.when(ti == pl.num_programs(1) - 1)
    def _():
        sfin_ref[...] = st_scr[...]


def _s5(x, g_mix, w_in, bd, cd, a_re, a_im, d_skip, w_glu, s0, tt):
    b, s, _ = x.shape
    n_state = N_SSM_BLOCKS * 2 * SSM_BLOCK_STATE
    x_spec = pl.BlockSpec((SUBLANES, tt, D_MODEL), lambda bi, ti: (bi, ti, 0))
    s_spec = pl.BlockSpec((None, SUBLANES, n_state), lambda bi, ti: (bi, 0, 0))
    rows = tt * SUBLANES
    return pl.pallas_call(
        functools.partial(_s5_body, tt=tt),
        grid=(b // SUBLANES, s // tt),
        in_specs=[x_spec, _resident((1, D_MODEL)), _resident((D_MODEL, D_MODEL)),
                  _resident(bd.shape), _resident(cd.shape), _resident(a_re.shape),
                  _resident(a_im.shape), _resident((1, D_MODEL)),
                  _resident((D_MODEL, 2 * D_MODEL)), s_spec],
        out_specs=[x_spec, s_spec],
        out_shape=[jax.ShapeDtypeStruct((b, s, D_MODEL), F32),
                   jax.ShapeDtypeStruct((b // SUBLANES, SUBLANES, n_state), F32)],
        scratch_shapes=[pltpu.VMEM((D_MODEL // LANES, rows, LANES), F32),
                        pltpu.VMEM((SUBLANES, n_state), F32),
                        pltpu.VMEM((rows, 2 * SSM_BLOCK_STATE), F32),
                        pltpu.VMEM((rows, D_MODEL), F32)],
        compiler_params=pltpu.CompilerParams(
            dimension_semantics=("arbitrary", "arbitrary"), vmem_limit_bytes=VMEM_LIMIT),
        name="s5_layer",
    )(x, g_mix, w_in, bd, cd, a_re, a_im, d_skip, w_glu, s0)


def _s5_tables(lam_re, lam_im, log_dt, b_re, b_im, c_re, c_im):
    dt = jnp.exp(log_dt)[:, None]
    mag = jnp.exp(lam_re * dt)
    a_re = mag * jnp.cos(lam_im * dt)
    a_im = mag * jnp.sin(lam_im * dt)
    den = lam_re * lam_re + lam_im * lam_im
    zr = ((a_re - 1.0) * lam_re + a_im * lam_im) / den
    zi = (a_im * lam_re - (a_re - 1.0) * lam_im) / den
    bbar_re = zr[..., None] * b_re - zi[..., None] * b_im
    bbar_im = zr[..., None] * b_im + zi[..., None] * b_re
    eye = jnp.eye(SSM_GROUPS_PER_BLOCK, dtype=F32)

    def pack_in(m):
        m = m.reshape(N_SSM_BLOCKS, SSM_GROUPS_PER_BLOCK, STATE_DIM, SSM_GROUP)
        full = jnp.einsum("kgpc,gh->kgchp", m, eye)
        return full.reshape(N_SSM_BLOCKS, SSM_BLOCK_IN, SSM_BLOCK_STATE)

    def pack_out(m):
        m = m.reshape(N_SSM_BLOCKS, SSM_GROUPS_PER_BLOCK, SSM_GROUP, STATE_DIM)
        full = jnp.einsum("kgcp,gh->kgphc", m, eye)
        return full.reshape(N_SSM_BLOCKS, SSM_BLOCK_STATE, SSM_BLOCK_IN)

    bd = jnp.concatenate([pack_in(bbar_re), pack_in(bbar_im)], axis=2).astype(BF16)
    cd = jnp.concatenate([pack_out(c_re), pack_out(-c_im)], axis=1).astype(BF16)
    shape = (N_SSM_BLOCKS, 1, SSM_BLOCK_STATE)
    return bd, cd, a_re.reshape(shape), a_im.reshape(shape)


def _pack_state(s_re, s_im):
    n = s_re.shape[0]
    packed = jnp.concatenate([s_re.reshape(n, N_SSM_BLOCKS, SSM_BLOCK_STATE),
                              s_im.reshape(n, N_SSM_BLOCKS, SSM_BLOCK_STATE)], axis=2)
    return packed.reshape(n // SUBLANES, SUBLANES, -1)


def _unpack_state(packed, n):
    s = packed.reshape(n, N_SSM_BLOCKS, 2, SSM_BLOCK_STATE)
    return (s[:, :, 0].reshape(n, N_GROUPS, STATE_DIM), s[:, :, 1].reshape(n, N_GROUPS, STATE_DIM))


def _trunk(x, w, tables, *, tm, tq, tt, past=None, ssm_prev=None, conv_prev=None):
    b, s, _ = x.shape
    t = b * s
    x2d = x.reshape(t, D_MODEL)

    shape3 = (b, s, D_MODEL)
    heads = (b, s, N_HEADS, HEAD_DIM)
    if past is None:
        q, kt, vt, ktb, vtb = _qkv_t(x2d, w["norm_mix"][0], w["wq"], w["wkt"], w["wvt"], w["q_gain"],
                                     w["k_gain_col"], s, tm)
        o = _attn_prompt(q.reshape(shape3), ktb, vtb, w["bias"] * LOG2_E, -w["tri"], tq, 2)
        feat_major = lambda a: jnp.transpose(a.reshape(b, N_HEADS, HEAD_DIM, s), (0, 3, 1, 2))
        k, v = feat_major(kt), feat_major(vt)
    else:
        q, k, v = _qkv(x2d, w["norm_mix"][0], w["wq"], w["wk"], w["wv"], w["q_gain"], w["k_gain"], tm)
        cache_kt, cache_vt, page_table, pps = past
        o = _attn_sample(q.reshape(shape3), k.reshape(shape3), v.reshape(shape3), w["bias_rows"],
                         w["tri"], cache_kt, cache_vt, page_table, pps)
        k, v = k.reshape(heads), v.reshape(heads)
    prev0 = None if conv_prev is None else conv_prev[0]
    x2d, conv0 = _ffn(x2d, o.reshape(t, D_MODEL), w["w_o"], w["norm_ffn"][0], w["w_up"][0],
                      w["conv_w"][0], w["conv_b"][0], w["w_down"][0], s, tm, prev0)

    bd, cd, a_re, a_im = tables
    if ssm_prev is None:
        s0 = jnp.zeros((b // SUBLANES, SUBLANES, N_SSM_BLOCKS * 2 * SSM_BLOCK_STATE), F32)
    else:
        s0 = _pack_state(*ssm_prev)
    x3, s_fin = _s5(x2d.reshape(shape3), w["norm_mix"][1], w["w_in"], bd, cd, a_re, a_im,
                    w["d_skip"], w["w_glu"], s0, tt)
    s_re, s_im = _unpack_state(s_fin, b)
    prev1 = None if conv_prev is None else conv_prev[1]
    y2d, conv1 = _ffn(x3.reshape(t, D_MODEL), None, None, w["norm_ffn"][1], w["w_up"][1],
                      w["conv_w"][1], w["conv_b"][1], w["w_down"][1], s, tm, prev1)

    def conv_state(c):
        if conv_prev is None:
            per_seq = c.reshape(b, s // tm, SUBLANES, D_FF)[:, -1]
        else:
            per_seq = c.reshape(b, s, D_FF)
        return per_seq[:, -2:]

    return (y2d.reshape(shape3), k[None], v[None], s_re[None], s_im[None],
            jnp.stack([conv_state(conv0), conv_state(conv1)]))


def _short_seq_prev_rows(prev, seq_len):
    bsz = prev.shape[0]
    zeros = jnp.zeros((bsz, seq_len - 2, D_FF), F32)
    p2 = jnp.concatenate([prev, zeros], axis=1)
    p1 = jnp.concatenate([prev[:, 1:2], jnp.zeros((bsz, seq_len - 1, D_FF), F32)], axis=1)
    return p1.reshape(bsz * seq_len, D_FF), p2.reshape(bsz * seq_len, D_FF)


def kernel(x_prompt, x_sample, cache_k, cache_v, state_ssm_re, state_ssm_im, state_ffn_conv, page_table,
           norm_mix, norm_ffn, attn_w_qkv, attn_q_gain, attn_k_gain, attn_logit_bias, attn_w_o,
           ssm_w_in, ssm_lambda_re, ssm_lambda_im, ssm_log_dt, ssm_b_re, ssm_b_im, ssm_c_re, ssm_c_im,
           ssm_d, ssm_w_glu, ffn_w_up, ffn_conv_w, ffn_conv_b, ffn_w_down):
    dec_batch, dec_seq, _ = x_sample.shape
    w = {
        "norm_mix": norm_mix[:, None, :], "norm_ffn": norm_ffn[:, None, :],
        "wq": attn_w_qkv[0, :, 0:D_MODEL].astype(BF16),
        "wk": attn_w_qkv[0, :, D_MODEL:2 * D_MODEL].astype(BF16),
        "wv": attn_w_qkv[0, :, 2 * D_MODEL:3 * D_MODEL].astype(BF16),
        "wkt": attn_w_qkv[0, :, D_MODEL:2 * D_MODEL].T.astype(BF16),
        "wvt": attn_w_qkv[0, :, 2 * D_MODEL:3 * D_MODEL].T.astype(BF16),
        "q_gain": jnp.tile(attn_q_gain[0], N_HEADS)[None], "k_gain": jnp.tile(attn_k_gain[0], N_HEADS)[None],
        "k_gain_col": attn_k_gain[0][:, None],
        "bias": attn_logit_bias[0],
        "bias_rows": jnp.repeat(attn_logit_bias[0], dec_seq)[:, None],
        "tri": jnp.tril(jnp.ones((MXU_DIM, MXU_DIM), BF16)),
        "w_o": attn_w_o[0].astype(BF16),
        "w_in": ssm_w_in[0].astype(BF16), "d_skip": ssm_d[0][None], "w_glu": ssm_w_glu[0].astype(BF16),
        "w_up": ffn_w_up.astype(BF16), "conv_w": ffn_conv_w, "conv_b": ffn_conv_b[:, None, :],
        "w_down": ffn_w_down.astype(BF16),
    }
    tables = _s5_tables(ssm_lambda_re[0], ssm_lambda_im[0], ssm_log_dt[0], ssm_b_re[0], ssm_b_im[0],
                        ssm_c_re[0], ssm_c_im[0])

    y_p, k_p, v_p, sr_p, si_p, conv_p = _trunk(x_prompt, w, tables, tm=512, tq=256, tt=64)

    n_pool = cache_k.shape[1]
    to_pages = lambda c: jnp.transpose(c[0], (0, 2, 3, 1)).reshape(n_pool, D_MODEL, PAGE_SIZE)
    past = (to_pages(cache_k), to_pages(cache_v), page_table, 8)
    conv_prev = [_short_seq_prev_rows(state_ffn_conv[i], dec_seq) for i in range(state_ffn_conv.shape[0])]
    y_s, k_s, v_s, sr_s, si_s, conv_s = _trunk(
        x_sample, w, tables, tm=dec_batch * dec_seq, tq=None, tt=dec_seq,
        past=past, ssm_prev=(state_ssm_re[0], state_ssm_im[0]), conv_prev=conv_prev)

    return (y_p, y_s, k_p, v_p, k_s, v_s, sr_p, si_p, sr_s, si_s, conv_p, conv_s)
```

```python
import functools
import math

import jax
import jax.numpy as jnp
from jax import lax
from jax.experimental import pallas as pl
from jax.experimental.pallas import tpu as pltpu

F32 = jnp.float32
BF16 = jnp.bfloat16

D_MODEL = 1024
N_HEADS = 16
HEAD_DIM = 64
PAGE_SIZE = 128
SSM_GROUP = 16
N_GROUPS = D_MODEL // SSM_GROUP
STATE_DIM = 64
D_FF = 2816
NORM_EPS = 1e-6

LANES = 128
SUBLANES = 8
MXU_DIM = 256
VMEM_LIMIT = 56 * 1024 * 1024

HEADS_PER_SLAB = LANES // HEAD_DIM
SSM_BLOCK_IN = MXU_DIM
SSM_GROUPS_PER_BLOCK = SSM_BLOCK_IN // SSM_GROUP
SSM_BLOCK_STATE = SSM_GROUPS_PER_BLOCK * STATE_DIM
N_SSM_BLOCKS = D_MODEL // SSM_BLOCK_IN
SSM_SCAN_LANES = 512
FF_CHUNK = MXU_DIM
LOG2_E = math.log2(math.e)
MASKED_LOGIT = -1e30


def _dot(a, b):
    return jnp.dot(a, b, preferred_element_type=F32)


def _dot_nt(a, b):
    return lax.dot_general(a, b, (((1,), (1,)), ((), ())), preferred_element_type=F32)


def _rms(x, gain):
    return x * lax.rsqrt(jnp.mean(x * x, axis=-1, keepdims=True) + NORM_EPS) * gain


def _resident(shape):
    nd = len(shape)
    return pl.BlockSpec(shape, lambda *_: (0,) * nd, pipeline_mode=pl.Buffered(1))


def _neg_softplus(z):
    return -(jnp.maximum(z, 0.0) + jnp.log(1.0 + jnp.exp(-jnp.abs(z))))


def _head_norm_rows(t, gain):
    r = lax.broadcasted_iota(jnp.int32, (MXU_DIM, MXU_DIM), 0) // HEAD_DIM
    c = lax.broadcasted_iota(jnp.int32, (MXU_DIM, MXU_DIM), 1) // HEAD_DIM
    seg = (r == c).astype(BF16)
    sq = (t * t).astype(BF16)
    ms = jnp.concatenate(
        [_dot(sq[:, j * MXU_DIM:(j + 1) * MXU_DIM], seg) for j in range(D_MODEL // MXU_DIM)],
        axis=1) * (1.0 / HEAD_DIM)
    return t * lax.rsqrt(ms + NORM_EPS) * gain


def _qkv_body(x_ref, gm_ref, wq_ref, wk_ref, wv_ref, qg_ref, kg_ref, q_ref, k_ref, v_ref):
    h = _rms(x_ref[...], gm_ref[...]).astype(BF16)
    q = _head_norm_rows(_dot(h, wq_ref[...]), qg_ref[...])
    q_ref[...] = (q * (HEAD_DIM ** -0.5)).astype(BF16)
    k_ref[...] = _head_norm_rows(_dot(h, wk_ref[...]), kg_ref[...])
    v_ref[...] = _dot(h, wv_ref[...])


def _qkv(x2d, g_mix, wq, wk, wv, q_gain, k_gain, tm):
    t = x2d.shape[0]
    tok = pl.BlockSpec((tm, D_MODEL), lambda i: (i, 0))
    wspec = _resident((D_MODEL, D_MODEL))
    return pl.pallas_call(
        _qkv_body,
        grid=(t // tm,),
        in_specs=[tok, _resident((1, D_MODEL)), wspec, wspec, wspec,
                  _resident((1, D_MODEL)), _resident((1, D_MODEL))],
        out_specs=[tok, tok, tok],
        out_shape=[jax.ShapeDtypeStruct((t, D_MODEL), BF16),
                   jax.ShapeDtypeStruct((t, D_MODEL), F32),
                   jax.ShapeDtypeStruct((t, D_MODEL), F32)],
        compiler_params=pltpu.CompilerParams(
            dimension_semantics=("arbitrary",), vmem_limit_bytes=VMEM_LIMIT),
        name="qkv_proj",
    )(x2d, g_mix, wq, wk, wv, q_gain, k_gain)


def _qkv_t_body(x_ref, gm_ref, wq_ref, wkt_ref, wvt_ref, qg_ref, kgt_ref,
                q_ref, kt_ref, vt_ref, ktb_ref, vtb_ref):
    tm = x_ref.shape[0]
    h = _rms(x_ref[...], gm_ref[...]).astype(BF16)
    q = _head_norm_rows(_dot(h, wq_ref[...]), qg_ref[...])
    q_ref[...] = (q * (HEAD_DIM ** -0.5 * LOG2_E)).astype(BF16)
    k3 = _dot_nt(wkt_ref[...], h).reshape(N_HEADS, HEAD_DIM, tm)
    ms = jnp.mean(k3 * k3, axis=1, keepdims=True)
    kt = (k3 * lax.rsqrt(ms + NORM_EPS) * kgt_ref[...][None]).reshape(D_MODEL, tm)
    kt_ref[...] = kt
    ktb_ref[...] = kt.astype(BF16)
    vt = _dot_nt(wvt_ref[...], h)
    vt_ref[...] = vt
    vtb_ref[...] = vt.astype(BF16)


def _qkv_t(x2d, g_mix, wq, wkt, wvt, q_gain, k_gain_col, seq_len, tm):
    t = x2d.shape[0]
    tiles_per_seq = seq_len // tm
    tok = pl.BlockSpec((tm, D_MODEL), lambda i: (i, 0))
    feat = pl.BlockSpec((None, D_MODEL, tm), lambda i: (i // tiles_per_seq, 0, i % tiles_per_seq))
    wspec = _resident((D_MODEL, D_MODEL))
    shape_t = (t // seq_len, D_MODEL, seq_len)
    return pl.pallas_call(
        _qkv_t_body,
        grid=(t // tm,),
        in_specs=[tok, _resident((1, D_MODEL)), wspec, wspec, wspec,
                  _resident((1, D_MODEL)), _resident((HEAD_DIM, 1))],
        out_specs=[tok, feat, feat, feat, feat],
        out_shape=[jax.ShapeDtypeStruct((t, D_MODEL), BF16),
                   jax.ShapeDtypeStruct(shape_t, F32),
                   jax.ShapeDtypeStruct(shape_t, F32),
                   jax.ShapeDtypeStruct(shape_t, BF16),
                   jax.ShapeDtypeStruct(shape_t, BF16)],
        compiler_params=pltpu.CompilerParams(
            dimension_semantics=("arbitrary",), vmem_limit_bytes=VMEM_LIMIT),
        name="qkv_proj_t",
    )(x2d, g_mix, wq, wkt, wvt, q_gain, k_gain_col)


def _attn_prompt_body(bias_ref, tri_ref, q_ref, kt_ref, vt_ref, o_ref, acc_ref, z_scr, i_scr,
                      *, tq, n_slabs):
    grp = pl.program_id(1)
    i = pl.program_id(2)
    n_heads = n_slabs * HEADS_PER_SLAB
    low = lax.broadcasted_iota(jnp.int32, (tq, LANES), 1) < HEAD_DIM
    q_heads = []
    for s in range(n_slabs):
        q = q_ref[:, s * LANES:(s + 1) * LANES]
        q_heads += [jnp.where(low, q, jnp.zeros_like(q)), jnp.where(low, jnp.zeros_like(q), q)]
    bias = [bias_ref[grp * n_heads + h] for h in range(n_heads)]
    row = lax.broadcasted_iota(jnp.int32, (tq, tq), 0)
    col = lax.broadcasted_iota(jnp.int32, (tq, tq), 1)
    causal = col < row

    acc_ref[...] = jnp.zeros_like(acc_ref)

    def rows_of(h):
        return slice((h // HEADS_PER_SLAB) * LANES, (h // HEADS_PER_SLAB + 1) * LANES)

    def stage_a(j, diag):
        keys = pl.ds(pl.multiple_of(j * tq, tq), tq)
        totals = []
        for h in range(n_heads):
            z = _dot(q_heads[h], kt_ref[rows_of(h), keys]) + bias[h]
            if diag:
                z = jnp.where(causal, z, MASKED_LOGIT)
            sp = jnp.maximum(z, 0.0) + jnp.log2(1.0 + jnp.exp2(-jnp.abs(z)))
            incl = _dot(sp.astype(BF16), tri_ref[...])
            z_scr[h] = z
            i_scr[h] = incl
            totals.append(incl[:, 0:1])
        return tuple(totals)

    def stage_b(j, carry):
        keys = pl.ds(pl.multiple_of(j * tq, tq), tq)
        for h in range(n_heads):
            w = jnp.exp2(z_scr[h] + i_scr[h] + carry[h])
            acc_ref[h] += _dot_nt(w.astype(BF16), vt_ref[rows_of(h), keys])

    def step(jj, state):
        carry, totals = state
        j = i - 1 - jj
        stage_b(j + 1, carry)
        carry = tuple(c + t for c, t in zip(carry, totals))
        return carry, stage_a(j, False)

    zero = jnp.zeros((tq, 1), F32)
    carry, _ = lax.fori_loop(0, i, step, ((zero,) * n_heads, stage_a(i, True)))
    stage_b(0, carry)
    for s in range(n_slabs):
        o_ref[:, s * LANES:(s + 1) * LANES] = jnp.where(
            low, acc_ref[HEADS_PER_SLAB * s], acc_ref[HEADS_PER_SLAB * s + 1]).astype(BF16)


def _attn_prompt(q, kt, vt, bias, tri, tq, n_slabs):
    b, s, _ = q.shape
    width = n_slabs * LANES
    kv_spec = pl.BlockSpec((None, width, s), lambda bi, gi, qi: (bi, gi, 0))
    q_spec = pl.BlockSpec((None, tq, width), lambda bi, gi, qi: (bi, qi, gi))
    return pl.pallas_call(
        functools.partial(_attn_prompt_body, tq=tq, n_slabs=n_slabs),
        grid=(b, D_MODEL // width, s // tq),
        in_specs=[pl.BlockSpec(memory_space=pltpu.SMEM),
                  pl.BlockSpec((tq, tq), lambda bi, gi, qi: (0, 0)), q_spec, kv_spec, kv_spec],
        out_specs=q_spec,
        out_shape=jax.ShapeDtypeStruct((b, s, D_MODEL), BF16),
        scratch_shapes=[pltpu.VMEM((n_slabs * HEADS_PER_SLAB, tq, LANES), F32),
                        pltpu.VMEM((n_slabs * HEADS_PER_SLAB, tq, tq), F32),
                        pltpu.VMEM((n_slabs * HEADS_PER_SLAB, tq, tq), F32)],
        compiler_params=pltpu.CompilerParams(
            dimension_semantics=("arbitrary", "arbitrary", "arbitrary"),
            vmem_limit_bytes=VMEM_LIMIT),
        name="attn_prompt",
    )(bias, tri, q, kt, vt)


def _attn_sample_body(pt_ref, q_ref, kn_ref, vn_ref, bias_ref, tri_ref, *rest, n_q, pps):
    del pt_ref
    kt_refs, vt_refs = rest[:pps], rest[pps:2 * pps]
    o_ref, qbd_ref, kt_scr, vt_scr, w_scr, acc_ref, carry_ref = rest[2 * pps:]
    step = pl.program_id(1)
    n_rows = N_HEADS * n_q
    window = pps * PAGE_SIZE

    @pl.when(step == 0)
    def _():
        q_rep = jnp.concatenate([q_ref[...].astype(F32)] * N_HEADS, axis=0)
        rh = lax.broadcasted_iota(jnp.int32, (n_rows, D_MODEL), 0) // n_q
        ch = lax.broadcasted_iota(jnp.int32, (n_rows, D_MODEL), 1) // HEAD_DIM
        qbd = jnp.where(rh == ch, q_rep, 0.0).astype(BF16)
        qbd_ref[...] = qbd
        pad = jnp.zeros((PAGE_SIZE - n_q, D_MODEL), F32)
        kn = jnp.concatenate([kn_ref[...], pad], axis=0).astype(BF16)
        vn = jnp.concatenate([vn_ref[...], pad], axis=0).astype(BF16)
        z = _dot_nt(qbd, kn) + bias_ref[...]
        key = lax.broadcasted_iota(jnp.int32, (n_rows, PAGE_SIZE), 1)
        qi = lax.broadcasted_iota(jnp.int32, (n_rows, PAGE_SIZE), 0) % n_q
        visible = key < qi
        lk = jnp.where(visible, _neg_softplus(z), 0.0)
        incl = _dot(lk.astype(BF16), tri_ref[0:PAGE_SIZE, 0:PAGE_SIZE])
        w = jnp.where(visible, jnp.exp(z + incl), 0.0)
        acc_ref[...] = _dot(w.astype(BF16), vn)
        carry_ref[...] = jnp.broadcast_to(incl[:, 0:1], carry_ref.shape)

    for i in range(pps):
        lanes = slice(i * PAGE_SIZE, (i + 1) * PAGE_SIZE)
        kt_scr[:, lanes] = kt_refs[i][...].astype(BF16)
        vt_scr[:, lanes] = vt_refs[i][...].astype(BF16)
    z_all = _dot(qbd_ref[...], kt_scr[...]) + bias_ref[...]
    carry = carry_ref[:, 0:1]
    for ch in reversed(range(window // MXU_DIM)):
        lanes = slice(ch * MXU_DIM, (ch + 1) * MXU_DIM)
        z = z_all[:, lanes]
        incl = _dot(_neg_softplus(z).astype(BF16), tri_ref[...])
        w_scr[:, lanes] = jnp.exp(z + incl + carry).astype(BF16)
        carry = carry + incl[:, 0:1]
    acc_ref[...] += _dot_nt(w_scr[...], vt_scr[...])
    carry_ref[...] = jnp.broadcast_to(carry, carry_ref.shape)

    @pl.when(step == pl.num_programs(1) - 1)
    def _():
        rh = lax.broadcasted_iota(jnp.int32, (n_rows, D_MODEL), 0) // n_q
        ch = lax.broadcasted_iota(jnp.int32, (n_rows, D_MODEL), 1) // HEAD_DIM
        own = jnp.where(rh == ch, acc_ref[...], 0.0)
        out = own[0:n_q, :]
        for h in range(1, N_HEADS):
            out = out + own[h * n_q:(h + 1) * n_q, :]
        o_ref[...] = out.astype(BF16)


def _attn_sample(q, k_new, v_new, bias_rows, tri, cache_kt, cache_vt, page_table, pps):
    db, n_q, _ = q.shape
    n_pages = page_table.shape[1]
    n_rows = N_HEADS * n_q
    window = pps * PAGE_SIZE

    def page_spec(i):
        def idx(bi, si, pt):
            return (pt[bi, n_pages - pps * (si + 1) + i], 0, 0)
        return pl.BlockSpec((None, D_MODEL, PAGE_SIZE), idx)

    per_seq = lambda bi, si, pt: (bi, 0, 0)
    const = lambda bi, si, pt: (0, 0)
    pages = [page_spec(i) for i in range(pps)]
    grid_spec = pltpu.PrefetchScalarGridSpec(
        num_scalar_prefetch=1,
        grid=(db, n_pages // pps),
        in_specs=[pl.BlockSpec((None, n_q, D_MODEL), per_seq),
                  pl.BlockSpec((None, n_q, D_MODEL), per_seq),
                  pl.BlockSpec((None, n_q, D_MODEL), per_seq),
                  pl.BlockSpec((n_rows, 1), const),
                  pl.BlockSpec((MXU_DIM, MXU_DIM), const)] + pages + pages,
        out_specs=pl.BlockSpec((None, n_q, D_MODEL), per_seq),
        scratch_shapes=[pltpu.VMEM((n_rows, D_MODEL), BF16),
                        pltpu.VMEM((D_MODEL, window), BF16),
                        pltpu.VMEM((D_MODEL, window), BF16),
                        pltpu.VMEM((n_rows, window), BF16),
                        pltpu.VMEM((n_rows, D_MODEL), F32),
                        pltpu.VMEM((n_rows, LANES), F32)])
    return pl.pallas_call(
        functools.partial(_attn_sample_body, n_q=n_q, pps=pps),
        grid_spec=grid_spec,
        out_shape=jax.ShapeDtypeStruct((db, n_q, D_MODEL), BF16),
        compiler_params=pltpu.CompilerParams(
            dimension_semantics=("arbitrary", "arbitrary"), vmem_limit_bytes=VMEM_LIMIT),
        name="attn_sample",
    )(page_table, q, k_new, v_new, bias_rows, tri, *([cache_kt] * pps), *([cache_vt] * pps))


def _ffn_body(*refs, with_wo, short_seq, tiles_per_seq, seq_len):
    refs = list(refs)
    x_ref = refs.pop(0)
    if with_wo:
        o_ref, wo_ref = refs.pop(0), refs.pop(0)
    g_ref, wup_ref, cw_ref, cb_ref, wdn_ref = [refs.pop(0) for _ in range(5)]
    if short_seq:
        p1_ref, p2_ref = refs.pop(0), refs.pop(0)
    out_ref, conv_ref, a_scr, g_scr = refs
    tm = x_ref.shape[0]

    x1 = x_ref[...]
    if with_wo:
        x1 = x1 + _dot(o_ref[...], wo_ref[...])
    h = _rms(x1, g_ref[...]).astype(BF16)

    if short_seq:
        a_scr[0:SUBLANES, :] = jnp.zeros((SUBLANES, D_FF), F32)
    else:
        @pl.when(pl.program_id(0) % tiles_per_seq == 0)
        def _():
            a_scr[0:SUBLANES, :] = jnp.zeros((SUBLANES, D_FF), F32)

    a_scr[SUBLANES:SUBLANES + tm, :] = _dot(h, wup_ref[:, 0:D_FF])

    row = lax.broadcasted_iota(jnp.int32, (tm, FF_CHUNK), 0)
    for c in range(D_FF // FF_CHUNK):
        cs = slice(c * FF_CHUNK, (c + 1) * FF_CHUNK)
        a0 = a_scr[SUBLANES:SUBLANES + tm, cs]
        a1 = a_scr[SUBLANES - 1:SUBLANES - 1 + tm, cs]
        a2 = a_scr[SUBLANES - 2:SUBLANES - 2 + tm, cs]
        if short_seq:
            pos = row % seq_len
            a1 = jnp.where(pos < 1, p1_ref[:, cs], a1)
            a2 = jnp.where(pos < 2, p2_ref[:, cs], a2)
        b = _dot(h, wup_ref[:, D_FF + c * FF_CHUNK:D_FF + (c + 1) * FF_CHUNK])
        cc = cb_ref[:, cs] + cw_ref[0:1, cs] * a2 + cw_ref[1:2, cs] * a1 + cw_ref[2:3, cs] * a0
        g_scr[:, cs] = (cc * jax.nn.sigmoid(cc) * b).astype(BF16)

    tail = a_scr[tm:tm + SUBLANES, :]
    if short_seq:
        conv_ref[...] = a_scr[SUBLANES:SUBLANES + tm, :]
    else:
        conv_ref[...] = tail
        a_scr[0:SUBLANES, :] = tail
    out_ref[...] = x1 + _dot(g_scr[...], wdn_ref[...])


def _ffn(x2d, o2d, w_o, g_ffn, w_up, conv_w, conv_b, w_down, seq_len, tm, prev_rows=None):
    t = x2d.shape[0]
    with_wo = o2d is not None
    short_seq = prev_rows is not None
    n_tiles = t // tm
    row = lambda i: (i, 0)
    tok = pl.BlockSpec((tm, D_MODEL), row)
    args, specs = [x2d], [tok]
    if with_wo:
        args += [o2d, w_o]
        specs += [tok, _resident((D_MODEL, D_MODEL))]
    args += [g_ffn, w_up, conv_w, conv_b, w_down]
    specs += [_resident((1, D_MODEL)), _resident((D_MODEL, 2 * D_FF)), _resident((3, D_FF)),
              _resident((1, D_FF)), _resident((D_FF, D_MODEL))]
    if short_seq:
        args += list(prev_rows)
        specs += [pl.BlockSpec((tm, D_FF), row)] * 2
        conv_shape, conv_spec = (t, D_FF), pl.BlockSpec((tm, D_FF), row)
        tiles_per_seq = 1
    else:
        conv_shape = (n_tiles, SUBLANES, D_FF)
        conv_spec = pl.BlockSpec((None, SUBLANES, D_FF), lambda i: (i, 0, 0))
        tiles_per_seq = seq_len // tm
    return pl.pallas_call(
        functools.partial(_ffn_body, with_wo=with_wo, short_seq=short_seq,
                          tiles_per_seq=tiles_per_seq, seq_len=seq_len),
        grid=(n_tiles,),
        in_specs=specs,
        out_specs=[tok, conv_spec],
        out_shape=[jax.ShapeDtypeStruct((t, D_MODEL), F32),
                   jax.ShapeDtypeStruct(conv_shape, F32)],
        scratch_shapes=[pltpu.VMEM((tm + SUBLANES, D_FF), F32), pltpu.VMEM((tm, D_FF), BF16)],
        compiler_params=pltpu.CompilerParams(
            dimension_semantics=("arbitrary",), vmem_limit_bytes=VMEM_LIMIT),
        name="conv_ffn",
    )(*args)


def _s5_body(x_ref, g_ref, win_ref, bd_ref, cd_ref, are_ref, aim_ref, dskip_ref, wglu_ref, s0_ref,
             xo_ref, sfin_ref, xt_scr, st_scr, bu_scr, y_scr, *, tt):
    ti = pl.program_id(1)
    n_seq = SUBLANES

    @pl.when(ti == 0)
    def _():
        st_scr[...] = s0_ref[...]

    n_slabs = D_MODEL // LANES
    for b in range(n_seq):
        for j in range(n_slabs):
            xt_scr[j, pl.ds(b, tt, stride=n_seq), :] = x_ref[b, :, j * LANES:(j + 1) * LANES]
    xt = jnp.concatenate([xt_scr[j] for j in range(n_slabs)], axis=1)
    u = _dot(_rms(xt, g_ref[...]).astype(BF16), win_ref[...])
    ub = u.astype(BF16)

    for k in range(N_SSM_BLOCKS):
        bu_scr[...] = _dot(ub[:, k * SSM_BLOCK_IN:(k + 1) * SSM_BLOCK_IN], bd_ref[k])
        for lc in range(SSM_BLOCK_STATE // SSM_SCAN_LANES):
            re_l = slice(lc * SSM_SCAN_LANES, (lc + 1) * SSM_SCAN_LANES)
            im_l = slice(SSM_BLOCK_STATE + lc * SSM_SCAN_LANES,
                         SSM_BLOCK_STATE + (lc + 1) * SSM_SCAN_LANES)
            base = k * 2 * SSM_BLOCK_STATE
            st_re = slice(base + re_l.start, base + re_l.stop)
            st_im = slice(base + im_l.start, base + im_l.stop)
            ar = jnp.broadcast_to(are_ref[k, :, re_l], (n_seq, SSM_SCAN_LANES))
            ai = jnp.broadcast_to(aim_ref[k, :, re_l], (n_seq, SSM_SCAN_LANES))

            def step(t, s, ar=ar, ai=ai, re_l=re_l, im_l=im_l):
                sr, si = s
                r0 = pl.multiple_of(t * n_seq, n_seq)
                nr = ar * sr - ai * si + bu_scr[pl.ds(r0, n_seq), re_l]
                ni = ar * si + ai * sr + bu_scr[pl.ds(r0, n_seq), im_l]
                bu_scr[pl.ds(r0, n_seq), re_l] = nr
                bu_scr[pl.ds(r0, n_seq), im_l] = ni
                return nr, ni

            sr, si = lax.fori_loop(0, tt, step, (st_scr[:, st_re], st_scr[:, st_im]), unroll=True)
            st_scr[:, st_re] = sr
            st_scr[:, st_im] = si
        y_scr[:, k * SSM_BLOCK_IN:(k + 1) * SSM_BLOCK_IN] = _dot(bu_scr[...].astype(BF16), cd_ref[k])

    y = y_scr[...] + dskip_ref[...] * u
    gg = _dot(jax.nn.gelu(y).astype(BF16), wglu_ref[...])
    x_new = xt + gg[:, 0:D_MODEL] * jax.nn.sigmoid(gg[:, D_MODEL:2 * D_MODEL])
    for j in range(n_slabs):
        xt_scr[j] = x_new[:, j * LANES:(j + 1) * LANES]
    for b in range(n_seq):
        for j in range(n_slabs):
            xo_ref[b, :, j * LANES:(j + 1) * LANES] = xt_scr[j, pl.ds(b, tt, stride=n_seq), :]

    @pl.when(ti == pl.num_programs(1) - 1)
    def _():
        sfin_ref[...] = st_scr[...]


def _s5(x, g_mix, w_in, bd, cd, a_re, a_im, d_skip, w_glu, s0, tt):
    b, s, _ = x.shape
    n_state = N_SSM_BLOCKS * 2 * SSM_BLOCK_STATE
    x_spec = pl.BlockSpec((SUBLANES, tt, D_MODEL), lambda bi, ti: (bi, ti, 0))
    s_spec = pl.BlockSpec((None, SUBLANES, n_state), lambda bi, ti: (bi, 0, 0))
    rows = tt * SUBLANES
    return pl.pallas_call(
        functools.partial(_s5_body, tt=tt),
        grid=(b // SUBLANES, s // tt),
        in_specs=[x_spec, _resident((1, D_MODEL)), _resident((D_MODEL, D_MODEL)),
                  _resident(bd.shape), _resident(cd.shape), _resident(a_re.shape),
                  _resident(a_im.shape), _resident((1, D_MODEL)),
                  _resident((D_MODEL, 2 * D_MODEL)), s_spec],
        out_specs=[x_spec, s_spec],
        out_shape=[jax.ShapeDtypeStruct((b, s, D_MODEL), F32),
                   jax.ShapeDtypeStruct((b // SUBLANES, SUBLANES, n_state), F32)],
        scratch_shapes=[pltpu.VMEM((D_MODEL // LANES, rows, LANES), F32),
                        pltpu.VMEM((SUBLANES, n_state), F32),
                        pltpu.VMEM((rows, 2 * SSM_BLOCK_STATE), F32),
                        pltpu.VMEM((rows, D_MODEL), F32)],
        compiler_params=pltpu.CompilerParams(
            dimension_semantics=("arbitrary", "arbitrary"), vmem_limit_bytes=VMEM_LIMIT),
        name="s5_layer",
    )(x, g_mix, w_in, bd, cd, a_re, a_im, d_skip, w_glu, s0)


def _s5_tables(lam_re, lam_im, log_dt, b_re, b_im, c_re, c_im):
    dt = jnp.exp(log_dt)[:, None]
    mag = jnp.exp(lam_re * dt)
    a_re = mag * jnp.cos(lam_im * dt)
    a_im = mag * jnp.sin(lam_im * dt)
    den = lam_re * lam_re + lam_im * lam_im
    zr = ((a_re - 1.0) * lam_re + a_im * lam_im) / den
    zi = (a_im * lam_re - (a_re - 1.0) * lam_im) / den
    bbar_re = zr[..., None] * b_re - zi[..., None] * b_im
    bbar_im = zr[..., None] * b_im + zi[..., None] * b_re
    eye = jnp.eye(SSM_GROUPS_PER_BLOCK, dtype=F32)

    def pack_in(m):
        m = m.reshape(N_SSM_BLOCKS, SSM_GROUPS_PER_BLOCK, STATE_DIM, SSM_GROUP)
        full = jnp.einsum("kgpc,gh->kgchp", m, eye)
        return full.reshape(N_SSM_BLOCKS, SSM_BLOCK_IN, SSM_BLOCK_STATE)

    def pack_out(m):
        m = m.reshape(N_SSM_BLOCKS, SSM_GROUPS_PER_BLOCK, SSM_GROUP, STATE_DIM)
        full = jnp.einsum("kgcp,gh->kgphc", m, eye)
        return full.reshape(N_SSM_BLOCKS, SSM_BLOCK_STATE, SSM_BLOCK_IN)

    bd = jnp.concatenate([pack_in(bbar_re), pack_in(bbar_im)], axis=2).astype(BF16)
    cd = jnp.concatenate([pack_out(c_re), pack_out(-c_im)], axis=1).astype(BF16)
    shape = (N_SSM_BLOCKS, 1, SSM_BLOCK_STATE)
    return bd, cd, a_re.reshape(shape), a_im.reshape(shape)


def _pack_state(s_re, s_im):
    n = s_re.shape[0]
    packed = jnp.concatenate([s_re.reshape(n, N_SSM_BLOCKS, SSM_BLOCK_STATE),
                              s_im.reshape(n, N_SSM_BLOCKS, SSM_BLOCK_STATE)], axis=2)
    return packed.reshape(n // SUBLANES, SUBLANES, -1)


def _unpack_state(packed, n):
    s = packed.reshape(n, N_SSM_BLOCKS, 2, SSM_BLOCK_STATE)
    return (s[:, :, 0].reshape(n, N_GROUPS, STATE_DIM), s[:, :, 1].reshape(n, N_GROUPS, STATE_DIM))


def _trunk(x, w, tables, *, tm, tq, tt, past=None, ssm_prev=None, conv_prev=None):
    b, s, _ = x.shape
    t = b * s
    x2d = x.reshape(t, D_MODEL)

    shape3 = (b, s, D_MODEL)
    heads = (b, s, N_HEADS, HEAD_DIM)
    if past is None:
        q, kt, vt, ktb, vtb = _qkv_t(x2d, w["norm_mix"][0], w["wq"], w["wkt"], w["wvt"], w["q_gain"],
                                     w["k_gain_col"], s, tm)
        o = _attn_prompt(q.reshape(shape3), ktb, vtb, w["bias"] * LOG2_E, -w["tri"], tq, 4)
        feat_major = lambda a: jnp.transpose(a.reshape(b, N_HEADS, HEAD_DIM, s), (0, 3, 1, 2))
        k, v = feat_major(kt), feat_major(vt)
    else:
        q, k, v = _qkv(x2d, w["norm_mix"][0], w["wq"], w["wk"], w["wv"], w["q_gain"], w["k_gain"], tm)
        cache_kt, cache_vt, page_table, pps = past
        o = _attn_sample(q.reshape(shape3), k.reshape(shape3), v.reshape(shape3), w["bias_rows"],
                         w["tri"], cache_kt, cache_vt, page_table, pps)
        k, v = k.reshape(heads), v.reshape(heads)
    prev0 = None if conv_prev is None else conv_prev[0]
    x2d, conv0 = _ffn(x2d, o.reshape(t, D_MODEL), w["w_o"], w["norm_ffn"][0], w["w_up"][0],
                      w["conv_w"][0], w["conv_b"][0], w["w_down"][0], s, tm, prev0)

    bd, cd, a_re, a_im = tables
    if ssm_prev is None:
        s0 = jnp.zeros((b // SUBLANES, SUBLANES, N_SSM_BLOCKS * 2 * SSM_BLOCK_STATE), F32)
    else:
        s0 = _pack_state(*ssm_prev)
    x3, s_fin = _s5(x2d.reshape(shape3), w["norm_mix"][1], w["w_in"], bd, cd, a_re, a_im,
                    w["d_skip"], w["w_glu"], s0, tt)
    s_re, s_im = _unpack_state(s_fin, b)
    prev1 = None if conv_prev is None else conv_prev[1]
    y2d, conv1 = _ffn(x3.reshape(t, D_MODEL), None, None, w["norm_ffn"][1], w["w_up"][1],
                      w["conv_w"][1], w["conv_b"][1], w["w_down"][1], s, tm, prev1)

    def conv_state(c):
        if conv_prev is None:
            per_seq = c.reshape(b, s // tm, SUBLANES, D_FF)[:, -1]
        else:
            per_seq = c.reshape(b, s, D_FF)
        return per_seq[:, -2:]

    return (y2d.reshape(shape3), k[None], v[None], s_re[None], s_im[None],
            jnp.stack([conv_state(conv0), conv_state(conv1)]))


def _short_seq_prev_rows(prev, seq_len):
    bsz = prev.shape[0]
    zeros = jnp.zeros((bsz, seq_len - 2, D_FF), F32)
    p2 = jnp.concatenate([prev, zeros], axis=1)
    p1 = jnp.concatenate([prev[:, 1:2], jnp.zeros((bsz, seq_len - 1, D_FF), F32)], axis=1)
    return p1.reshape(bsz * seq_len, D_FF), p2.reshape(bsz * seq_len, D_FF)


def kernel(x_prompt, x_sample, cache_k, cache_v, state_ssm_re, state_ssm_im, state_ffn_conv, page_table,
           norm_mix, norm_ffn, attn_w_qkv, attn_q_gain, attn_k_gain, attn_logit_bias, attn_w_o,
           ssm_w_in, ssm_lambda_re, ssm_lambda_im, ssm_log_dt, ssm_b_re, ssm_b_im, ssm_c_re, ssm_c_im,
           ssm_d, ssm_w_glu, ffn_w_up, ffn_conv_w, ffn_conv_b, ffn_w_down):
    dec_batch, dec_seq, _ = x_sample.shape
    w = {
        "norm_mix": norm_mix[:, None, :], "norm_ffn": norm_ffn[:, None, :],
        "wq": attn_w_qkv[0, :, 0:D_MODEL].astype(BF16),
        "wk": attn_w_qkv[0, :, D_MODEL:2 * D_MODEL].astype(BF16),
        "wv": attn_w_qkv[0, :, 2 * D_MODEL:3 * D_MODEL].astype(BF16),
        "wkt": attn_w_qkv[0, :, D_MODEL:2 * D_MODEL].T.astype(BF16),
        "wvt": attn_w_qkv[0, :, 2 * D_MODEL:3 * D_MODEL].T.astype(BF16),
        "q_gain": jnp.tile(attn_q_gain[0], N_HEADS)[None], "k_gain": jnp.tile(attn_k_gain[0], N_HEADS)[None],
        "k_gain_col": attn_k_gain[0][:, None],
        "bias": attn_logit_bias[0],
        "bias_rows": jnp.repeat(attn_logit_bias[0], dec_seq)[:, None],
        "tri": jnp.tril(jnp.ones((MXU_DIM, MXU_DIM), BF16)),
        "w_o": attn_w_o[0].astype(BF16),
        "w_in": ssm_w_in[0].astype(BF16), "d_skip": ssm_d[0][None], "w_glu": ssm_w_glu[0].astype(BF16),
        "w_up": ffn_w_up.astype(BF16), "conv_w": ffn_conv_w, "conv_b": ffn_conv_b[:, None, :],
        "w_down": ffn_w_down.astype(BF16),
    }
    tables = _s5_tables(ssm_lambda_re[0], ssm_lambda_im[0], ssm_log_dt[0], ssm_b_re[0], ssm_b_im[0],
                        ssm_c_re[0], ssm_c_im[0])

    y_p, k_p, v_p, sr_p, si_p, conv_p = _trunk(x_prompt, w, tables, tm=512, tq=256, tt=64)

    n_pool = cache_k.shape[1]
    to_pages = lambda c: jnp.transpose(c[0], (0, 2, 3, 1)).reshape(n_pool, D_MODEL, PAGE_SIZE)
    past = (to_pages(cache_k), to_pages(cache_v), page_table, 16)
    conv_prev = [_short_seq_prev_rows(state_ffn_conv[i], dec_seq) for i in range(state_ffn_conv.shape[0])]
    y_s, k_s, v_s, sr_s, si_s, conv_s = _trunk(
        x_sample, w, tables, tm=dec_batch * dec_seq, tq=None, tt=dec_seq,
        past=past, ssm_prev=(state_ssm_re[0], state_ssm_im[0]), conv_prev=conv_prev)

    return (y_p, y_s, k_p, v_p, k_s, v_s, sr_p, si_p, sr_s, si_s, conv_p, conv_s)
```

```python
import functools
import math

import jax
import jax.numpy as jnp
from jax import lax
from jax.experimental import pallas as pl
from jax.experimental.pallas import tpu as pltpu

F32 = jnp.float32
BF16 = jnp.bfloat16

D_MODEL = 1024
N_HEADS = 16
HEAD_DIM = 64
PAGE_SIZE = 128
SSM_GROUP = 16
N_GROUPS = D_MODEL // SSM_GROUP
STATE_DIM = 64
D_FF = 2816
NORM_EPS = 1e-6

LANES = 128
SUBLANES = 8
MXU_DIM = 256
VMEM_LIMIT = 56 * 1024 * 1024

HEADS_PER_SLAB = LANES // HEAD_DIM
SSM_BLOCK_IN = MXU_DIM
SSM_GROUPS_PER_BLOCK = SSM_BLOCK_IN // SSM_GROUP
SSM_BLOCK_STATE = SSM_GROUPS_PER_BLOCK * STATE_DIM
N_SSM_BLOCKS = D_MODEL // SSM_BLOCK_IN
SSM_SCAN_LANES = 512
FF_CHUNK = MXU_DIM
LOG2_E = math.log2(math.e)
MASKED_LOGIT = -1e30


def _dot(a, b):
    return jnp.dot(a, b, preferred_element_type=F32)


def _dot_nt(a, b):
    return lax.dot_general(a, b, (((1,), (1,)), ((), ())), preferred_element_type=F32)


def _rms(x, gain):
    return x * lax.rsqrt(jnp.mean(x * x, axis=-1, keepdims=True) + NORM_EPS) * gain


def _resident(shape):
    nd = len(shape)
    return pl.BlockSpec(shape, lambda *_: (0,) * nd, pipeline_mode=pl.Buffered(1))


def _neg_softplus(z):
    return -(jnp.maximum(z, 0.0) + jnp.log(1.0 + jnp.exp(-jnp.abs(z))))


def _head_norm_rows(t, gain):
    r = lax.broadcasted_iota(jnp.int32, (MXU_DIM, MXU_DIM), 0) // HEAD_DIM
    c = lax.broadcasted_iota(jnp.int32, (MXU_DIM, MXU_DIM), 1) // HEAD_DIM
    seg = (r == c).astype(BF16)
    sq = (t * t).astype(BF16)
    ms = jnp.concatenate(
        [_dot(sq[:, j * MXU_DIM:(j + 1) * MXU_DIM], seg) for j in range(D_MODEL // MXU_DIM)],
        axis=1) * (1.0 / HEAD_DIM)
    return t * lax.rsqrt(ms + NORM_EPS) * gain


def _qkv_body(x_ref, gm_ref, wq_ref, wk_ref, wv_ref, qg_ref, kg_ref, q_ref, k_ref, v_ref):
    h = _rms(x_ref[...], gm_ref[...]).astype(BF16)
    q = _head_norm_rows(_dot(h, wq_ref[...]), qg_ref[...])
    q_ref[...] = (q * (HEAD_DIM ** -0.5)).astype(BF16)
    k_ref[...] = _head_norm_rows(_dot(h, wk_ref[...]), kg_ref[...])
    v_ref[...] = _dot(h, wv_ref[...])


def _qkv(x2d, g_mix, wq, wk, wv, q_gain, k_gain, tm):
    t = x2d.shape[0]
    tok = pl.BlockSpec((tm, D_MODEL), lambda i: (i, 0))
    wspec = _resident((D_MODEL, D_MODEL))
    return pl.pallas_call(
        _qkv_body,
        grid=(t // tm,),
        in_specs=[tok, _resident((1, D_MODEL)), wspec, wspec, wspec,
                  _resident((1, D_MODEL)), _resident((1, D_MODEL))],
        out_specs=[tok, tok, tok],
        out_shape=[jax.ShapeDtypeStruct((t, D_MODEL), BF16),
                   jax.ShapeDtypeStruct((t, D_MODEL), F32),
                   jax.ShapeDtypeStruct((t, D_MODEL), F32)],
        compiler_params=pltpu.CompilerParams(
            dimension_semantics=("arbitrary",), vmem_limit_bytes=VMEM_LIMIT),
        name="qkv_proj",
    )(x2d, g_mix, wq, wk, wv, q_gain, k_gain)


def _qkv_t_body(x_ref, gm_ref, wq_ref, wkt_ref, wvt_ref, qg_ref, kgt_ref,
                q_ref, kt_ref, vt_ref, ktb_ref, vtb_ref):
    tm = x_ref.shape[0]
    h = _rms(x_ref[...], gm_ref[...]).astype(BF16)
    q = _head_norm_rows(_dot(h, wq_ref[...]), qg_ref[...])
    q_ref[...] = (q * (HEAD_DIM ** -0.5 * LOG2_E)).astype(BF16)
    k3 = _dot_nt(wkt_ref[...], h).reshape(N_HEADS, HEAD_DIM, tm)
    ms = jnp.mean(k3 * k3, axis=1, keepdims=True)
    kt = (k3 * lax.rsqrt(ms + NORM_EPS) * kgt_ref[...][None]).reshape(D_MODEL, tm)
    kt_ref[...] = kt
    ktb_ref[...] = kt.astype(BF16)
    vt = _dot_nt(wvt_ref[...], h)
    vt_ref[...] = vt
    vtb_ref[...] = vt.astype(BF16)


def _qkv_t(x2d, g_mix, wq, wkt, wvt, q_gain, k_gain_col, seq_len, tm):
    t = x2d.shape[0]
    tiles_per_seq = seq_len // tm
    tok = pl.BlockSpec((tm, D_MODEL), lambda i: (i, 0))
    feat = pl.BlockSpec((None, D_MODEL, tm), lambda i: (i // tiles_per_seq, 0, i % tiles_per_seq))
    wspec = _resident((D_MODEL, D_MODEL))
    shape_t = (t // seq_len, D_MODEL, seq_len)
    return pl.pallas_call(
        _qkv_t_body,
        grid=(t // tm,),
        in_specs=[tok, _resident((1, D_MODEL)), wspec, wspec, wspec,
                  _resident((1, D_MODEL)), _resident((HEAD_DIM, 1))],
        out_specs=[tok, feat, feat, feat, feat],
        out_shape=[jax.ShapeDtypeStruct((t, D_MODEL), BF16),
                   jax.ShapeDtypeStruct(shape_t, F32),
                   jax.ShapeDtypeStruct(shape_t, F32),
                   jax.ShapeDtypeStruct(shape_t, BF16),
                   jax.ShapeDtypeStruct(shape_t, BF16)],
        compiler_params=pltpu.CompilerParams(
            dimension_semantics=("arbitrary",), vmem_limit_bytes=VMEM_LIMIT),
        name="qkv_proj_t",
    )(x2d, g_mix, wq, wkt, wvt, q_gain, k_gain_col)


def _prompt_attn_step(grp, i, bias_ref, tri_ref, q_ref, kt_ref, vt_ref, o_ref, acc_ref, z_scr, i_scr,
                      *, tq, n_slabs):
    n_heads = n_slabs * HEADS_PER_SLAB
    low = lax.broadcasted_iota(jnp.int32, (tq, LANES), 1) < HEAD_DIM
    q_heads = []
    for s in range(n_slabs):
        q = q_ref[:, s * LANES:(s + 1) * LANES]
        q_heads += [jnp.where(low, q, jnp.zeros_like(q)), jnp.where(low, jnp.zeros_like(q), q)]
    bias = [bias_ref[grp * n_heads + h] for h in range(n_heads)]
    row = lax.broadcasted_iota(jnp.int32, (tq, tq), 0)
    col = lax.broadcasted_iota(jnp.int32, (tq, tq), 1)
    causal = col < row

    acc_ref[...] = jnp.zeros_like(acc_ref)

    def rows_of(h):
        return slice((h // HEADS_PER_SLAB) * LANES, (h // HEADS_PER_SLAB + 1) * LANES)

    def stage_a(j, diag):
        keys = pl.ds(pl.multiple_of(j * tq, tq), tq)
        totals = []
        for h in range(n_heads):
            z = _dot(q_heads[h], kt_ref[rows_of(h), keys]) + bias[h]
            if diag:
                z = jnp.where(causal, z, MASKED_LOGIT)
            sp = jnp.maximum(z, 0.0) + jnp.log2(1.0 + jnp.exp2(-jnp.abs(z)))
            incl = _dot(sp.astype(BF16), tri_ref[...])
            z_scr[h] = z
            i_scr[h] = incl
            totals.append(incl[:, 0:1])
        return tuple(totals)

    def stage_b(j, carry):
        keys = pl.ds(pl.multiple_of(j * tq, tq), tq)
        for h in range(n_heads):
            w = jnp.exp2(z_scr[h] + i_scr[h] + carry[h])
            acc_ref[h] += _dot_nt(w.astype(BF16), vt_ref[rows_of(h), keys])

    def step(jj, state):
        carry, totals = state
        j = i - 1 - jj
        stage_b(j + 1, carry)
        carry = tuple(c + t for c, t in zip(carry, totals))
        return carry, stage_a(j, False)

    zero = jnp.zeros((tq, 1), F32)
    carry, _ = lax.fori_loop(0, i, step, ((zero,) * n_heads, stage_a(i, True)))
    stage_b(0, carry)
    for s in range(n_slabs):
        o_ref[:, s * LANES:(s + 1) * LANES] = jnp.where(
            low, acc_ref[HEADS_PER_SLAB * s], acc_ref[HEADS_PER_SLAB * s + 1]).astype(BF16)


def _sample_attn_step(step, q_ref, kn_ref, vn_ref, bias_ref, tri_ref, kt_refs, vt_refs,
                      qbd_ref, kt_scr, vt_scr, w_scr, acc_ref, carry_ref, *, n_q, pps):
    n_rows = N_HEADS * n_q
    window = pps * PAGE_SIZE

    @pl.when(step == 0)
    def _():
        q_rep = jnp.concatenate([q_ref[...].astype(F32)] * N_HEADS, axis=0)
        rh = lax.broadcasted_iota(jnp.int32, (n_rows, D_MODEL), 0) // n_q
        ch = lax.broadcasted_iota(jnp.int32, (n_rows, D_MODEL), 1) // HEAD_DIM
        qbd = jnp.where(rh == ch, q_rep, 0.0).astype(BF16)
        qbd_ref[...] = qbd
        pad = jnp.zeros((PAGE_SIZE - n_q, D_MODEL), F32)
        kn = jnp.concatenate([kn_ref[...], pad], axis=0).astype(BF16)
        vn = jnp.concatenate([vn_ref[...], pad], axis=0).astype(BF16)
        z = _dot_nt(qbd, kn) + bias_ref[...]
        key = lax.broadcasted_iota(jnp.int32, (n_rows, PAGE_SIZE), 1)
        qi = lax.broadcasted_iota(jnp.int32, (n_rows, PAGE_SIZE), 0) % n_q
        visible = key < qi
        lk = jnp.where(visible, _neg_softplus(z), 0.0)
        incl = _dot(lk.astype(BF16), tri_ref[0:PAGE_SIZE, 0:PAGE_SIZE])
        w = jnp.where(visible, jnp.exp(z + incl), 0.0)
        acc_ref[...] = _dot(w.astype(BF16), vn)
        carry_ref[...] = jnp.broadcast_to(incl[:, 0:1], carry_ref.shape)

    for i in range(pps):
        lanes = slice(i * PAGE_SIZE, (i + 1) * PAGE_SIZE)
        kt_scr[:, lanes] = kt_refs[i][...].astype(BF16)
        vt_scr[:, lanes] = vt_refs[i][...].astype(BF16)
    z_all = _dot(qbd_ref[...], kt_scr[...]) + bias_ref[...]
    carry = carry_ref[:, 0:1]
    for ch in reversed(range(window // MXU_DIM)):
        lanes = slice(ch * MXU_DIM, (ch + 1) * MXU_DIM)
        z = z_all[:, lanes]
        incl = _dot(_neg_softplus(z).astype(BF16), tri_ref[...])
        w_scr[:, lanes] = jnp.exp(z + incl + carry).astype(BF16)
        carry = carry + incl[:, 0:1]
    acc_ref[...] += _dot_nt(w_scr[...], vt_scr[...])
    carry_ref[...] = jnp.broadcast_to(carry, carry_ref.shape)


def _sample_attn_finish(o_ref, acc_ref, *, n_q):
    n_rows = N_HEADS * n_q
    rh = lax.broadcasted_iota(jnp.int32, (n_rows, D_MODEL), 0) // n_q
    ch = lax.broadcasted_iota(jnp.int32, (n_rows, D_MODEL), 1) // HEAD_DIM
    own = jnp.where(rh == ch, acc_ref[...], 0.0)
    out = own[0:n_q, :]
    for h in range(1, N_HEADS):
        out = out + own[h * n_q:(h + 1) * n_q, :]
    o_ref[...] = out.astype(BF16)


def _attn_body(pt_ref, bias_ref, ptri_ref, q_ref, kt_ref, vt_ref,
               qs_ref, kn_ref, vn_ref, srows_ref, stri_ref, *rest, tq, n_slabs, n_q, pps):
    del pt_ref
    kt_pages, vt_pages = rest[:pps], rest[pps:2 * pps]
    (o_ref, os_ref, acc_ref, z_scr, i_scr,
     qbd_ref, kt_scr, vt_scr, w_scr, sacc_ref, carry_ref) = rest[2 * pps:]
    grp = pl.program_id(1)
    i = pl.program_id(2)
    is_last = i == pl.num_programs(2) - 1
    _sample_attn_step(i, qs_ref, kn_ref, vn_ref, srows_ref, stri_ref, kt_pages, vt_pages,
                      qbd_ref, kt_scr, vt_scr, w_scr, sacc_ref, carry_ref, n_q=n_q, pps=pps)
    _prompt_attn_step(grp, i, bias_ref, ptri_ref, q_ref, kt_ref, vt_ref, o_ref, acc_ref, z_scr, i_scr,
                      tq=tq, n_slabs=n_slabs)
    pl.when(is_last)(functools.partial(_sample_attn_finish, os_ref, sacc_ref, n_q=n_q))


def _attention(q, kt, vt, bias, ptri, qs, k_new, v_new, bias_rows, stri, cache_kt, cache_vt, page_table,
               tq, n_slabs, pps):
    b, s, _ = q.shape
    db, n_q, _ = qs.shape
    n_pages = page_table.shape[1]
    width = n_slabs * LANES
    n_groups = D_MODEL // width
    n_rows = N_HEADS * n_q
    window = pps * PAGE_SIZE
    assert db == b * n_groups and n_pages // pps == s // tq and n_pages % pps == 0

    seq_of = lambda bi, gi: bi * n_groups + gi

    def page_spec(slot):
        def idx(bi, gi, qi, pt):
            return (pt[seq_of(bi, gi), n_pages - pps * (qi + 1) + slot], 0, 0)
        return pl.BlockSpec((None, D_MODEL, PAGE_SIZE), idx)

    kv_spec = pl.BlockSpec((None, width, s), lambda bi, gi, qi, pt: (bi, gi, 0))
    q_spec = pl.BlockSpec((None, tq, width), lambda bi, gi, qi, pt: (bi, qi, gi))
    per_seq = pl.BlockSpec((None, n_q, D_MODEL), lambda bi, gi, qi, pt: (seq_of(bi, gi), 0, 0))
    const = lambda bi, gi, qi, pt: (0, 0)
    pages = [page_spec(slot) for slot in range(pps)]
    n_heads = n_slabs * HEADS_PER_SLAB
    grid_spec = pltpu.PrefetchScalarGridSpec(
        num_scalar_prefetch=1,
        grid=(b, n_groups, s // tq),
        in_specs=[pl.BlockSpec(memory_space=pltpu.SMEM),
                  pl.BlockSpec((tq, tq), const), q_spec, kv_spec, kv_spec,
                  per_seq, per_seq, per_seq,
                  pl.BlockSpec((n_rows, 1), const),
                  pl.BlockSpec((MXU_DIM, MXU_DIM), const)] + pages + pages,
        out_specs=[q_spec, per_seq],
        scratch_shapes=[pltpu.VMEM((n_heads, tq, LANES), F32),
                        pltpu.VMEM((n_heads, tq, tq), F32),
                        pltpu.VMEM((n_heads, tq, tq), F32),
                        pltpu.VMEM((n_rows, D_MODEL), BF16),
                        pltpu.VMEM((D_MODEL, window), BF16),
                        pltpu.VMEM((D_MODEL, window), BF16),
                        pltpu.VMEM((n_rows, window), BF16),
                        pltpu.VMEM((n_rows, D_MODEL), F32),
                        pltpu.VMEM((n_rows, LANES), F32)])
    return pl.pallas_call(
        functools.partial(_attn_body, tq=tq, n_slabs=n_slabs, n_q=n_q, pps=pps),
        grid_spec=grid_spec,
        out_shape=[jax.ShapeDtypeStruct((b, s, D_MODEL), BF16),
                   jax.ShapeDtypeStruct((db, n_q, D_MODEL), BF16)],
        compiler_params=pltpu.CompilerParams(
            dimension_semantics=("arbitrary", "arbitrary", "arbitrary"),
            vmem_limit_bytes=VMEM_LIMIT),
        name="attention",
    )(page_table, bias, ptri, q, kt, vt, qs, k_new, v_new, bias_rows, stri,
      *([cache_kt] * pps), *([cache_vt] * pps))


def _ffn_body(*refs, with_wo, short_seq, tiles_per_seq, seq_len):
    refs = list(refs)
    x_ref = refs.pop(0)
    if with_wo:
        o_ref, wo_ref = refs.pop(0), refs.pop(0)
    g_ref, wup_ref, cw_ref, cb_ref, wdn_ref = [refs.pop(0) for _ in range(5)]
    if short_seq:
        p1_ref, p2_ref = refs.pop(0), refs.pop(0)
    out_ref, conv_ref, a_scr, g_scr = refs
    tm = x_ref.shape[0]

    x1 = x_ref[...]
    if with_wo:
        x1 = x1 + _dot(o_ref[...], wo_ref[...])
    h = _rms(x1, g_ref[...]).astype(BF16)

    if short_seq:
        a_scr[0:SUBLANES, :] = jnp.zeros((SUBLANES, D_FF), F32)
    else:
        @pl.when(pl.program_id(0) % tiles_per_seq == 0)
        def _():
            a_scr[0:SUBLANES, :] = jnp.zeros((SUBLANES, D_FF), F32)

    a_scr[SUBLANES:SUBLANES + tm, :] = _dot(h, wup_ref[:, 0:D_FF])

    row = lax.broadcasted_iota(jnp.int32, (tm, FF_CHUNK), 0)
    for c in range(D_FF // FF_CHUNK):
        cs = slice(c * FF_CHUNK, (c + 1) * FF_CHUNK)
        a0 = a_scr[SUBLANES:SUBLANES + tm, cs]
        a1 = a_scr[SUBLANES - 1:SUBLANES - 1 + tm, cs]
        a2 = a_scr[SUBLANES - 2:SUBLANES - 2 + tm, cs]
        if short_seq:
            pos = row % seq_len
            a1 = jnp.where(pos < 1, p1_ref[:, cs], a1)
            a2 = jnp.where(pos < 2, p2_ref[:, cs], a2)
        b = _dot(h, wup_ref[:, D_FF + c * FF_CHUNK:D_FF + (c + 1) * FF_CHUNK])
        cc = cb_ref[:, cs] + cw_ref[0:1, cs] * a2 + cw_ref[1:2, cs] * a1 + cw_ref[2:3, cs] * a0
        g_scr[:, cs] = (cc * jax.nn.sigmoid(cc) * b).astype(BF16)

    tail = a_scr[tm:tm + SUBLANES, :]
    if short_seq:
        conv_ref[...] = a_scr[SUBLANES:SUBLANES + tm, :]
    else:
        conv_ref[...] = tail
        a_scr[0:SUBLANES, :] = tail
    out_ref[...] = x1 + _dot(g_scr[...], wdn_ref[...])


def _ffn(x2d, o2d, w_o, g_ffn, w_up, conv_w, conv_b, w_down, seq_len, tm, prev_rows=None):
    t = x2d.shape[0]
    with_wo = o2d is not None
    short_seq = prev_rows is not None
    n_tiles = t // tm
    row = lambda i: (i, 0)
    tok = pl.BlockSpec((tm, D_MODEL), row)
    args, specs = [x2d], [tok]
    if with_wo:
        args += [o2d, w_o]
        specs += [tok, _resident((D_MODEL, D_MODEL))]
    args += [g_ffn, w_up, conv_w, conv_b, w_down]
    specs += [_resident((1, D_MODEL)), _resident((D_MODEL, 2 * D_FF)), _resident((3, D_FF)),
              _resident((1, D_FF)), _resident((D_FF, D_MODEL))]
    if short_seq:
        args += list(prev_rows)
        specs += [pl.BlockSpec((tm, D_FF), row)] * 2
        conv_shape, conv_spec = (t, D_FF), pl.BlockSpec((tm, D_FF), row)
        tiles_per_seq = 1
    else:
        conv_shape = (n_tiles, SUBLANES, D_FF)
        conv_spec = pl.BlockSpec((None, SUBLANES, D_FF), lambda i: (i, 0, 0))
        tiles_per_seq = seq_len // tm
    return pl.pallas_call(
        functools.partial(_ffn_body, with_wo=with_wo, short_seq=short_seq,
                          tiles_per_seq=tiles_per_seq, seq_len=seq_len),
        grid=(n_tiles,),
        in_specs=specs,
        out_specs=[tok, conv_spec],
        out_shape=[jax.ShapeDtypeStruct((t, D_MODEL), F32),
                   jax.ShapeDtypeStruct(conv_shape, F32)],
        scratch_shapes=[pltpu.VMEM((tm + SUBLANES, D_FF), F32), pltpu.VMEM((tm, D_FF), BF16)],
        compiler_params=pltpu.CompilerParams(
            dimension_semantics=("arbitrary",), vmem_limit_bytes=VMEM_LIMIT),
        name="conv_ffn",
    )(*args)


def _s5_body(x_ref, g_ref, win_ref, bd_ref, cd_ref, are_ref, aim_ref, dskip_ref, wglu_ref, s0_ref,
             xo_ref, sfin_ref, xt_scr, st_scr, bu_scr, y_scr, *, tt):
    ti = pl.program_id(1)
    n_seq = SUBLANES

    @pl.when(ti == 0)
    def _():
        st_scr[...] = s0_ref[...]

    n_slabs = D_MODEL // LANES
    for b in range(n_seq):
        for j in range(n_slabs):
            xt_scr[j, pl.ds(b, tt, stride=n_seq), :] = x_ref[b, :, j * LANES:(j + 1) * LANES]
    xt = jnp.concatenate([xt_scr[j] for j in range(n_slabs)], axis=1)
    u = _dot(_rms(xt, g_ref[...]).astype(BF16), win_ref[...])
    ub = u.astype(BF16)

    for k in range(N_SSM_BLOCKS):
        bu_scr[...] = _dot(ub[:, k * SSM_BLOCK_IN:(k + 1) * SSM_BLOCK_IN], bd_ref[k])
        for lc in range(SSM_BLOCK_STATE // SSM_SCAN_LANES):
            re_l = slice(lc * SSM_SCAN_LANES, (lc + 1) * SSM_SCAN_LANES)
            im_l = slice(SSM_BLOCK_STATE + lc * SSM_SCAN_LANES,
                         SSM_BLOCK_STATE + (lc + 1) * SSM_SCAN_LANES)
            base = k * 2 * SSM_BLOCK_STATE
            st_re = slice(base + re_l.start, base + re_l.stop)
            st_im = slice(base + im_l.start, base + im_l.stop)
            ar = jnp.broadcast_to(are_ref[k, :, re_l], (n_seq, SSM_SCAN_LANES))
            ai = jnp.broadcast_to(aim_ref[k, :, re_l], (n_seq, SSM_SCAN_LANES))

            def step(t, s, ar=ar, ai=ai, re_l=re_l, im_l=im_l):
                sr, si = s
                r0 = pl.multiple_of(t * n_seq, n_seq)
                nr = ar * sr - ai * si + bu_scr[pl.ds(r0, n_seq), re_l]
                ni = ar * si + ai * sr + bu_scr[pl.ds(r0, n_seq), im_l]
                bu_scr[pl.ds(r0, n_seq), re_l] = nr
                bu_scr[pl.ds(r0, n_seq), im_l] = ni
                return nr, ni

            sr, si = lax.fori_loop(0, tt, step, (st_scr[:, st_re], st_scr[:, st_im]), unroll=True)
            st_scr[:, st_re] = sr
            st_scr[:, st_im] = si
        y_scr[:, k * SSM_BLOCK_IN:(k + 1) * SSM_BLOCK_IN] = _dot(bu_scr[...].astype(BF16), cd_ref[k])

    y = y_scr[...] + dskip_ref[...] * u
    gg = _dot(jax.nn.gelu(y).astype(BF16), wglu_ref[...])
    x_new = xt + gg[:, 0:D_MODEL] * jax.nn.sigmoid(gg[:, D_MODEL:2 * D_MODEL])
    for j in range(n_slabs):
        xt_scr[j] = x_new[:, j * LANES:(j + 1) * LANES]
    for b in range(n_seq):
        for j in range(n_slabs):
            xo_ref[b, :, j * LANES:(j + 1) * LANES] = xt_scr[j, pl.ds(b, tt, stride=n_seq), :]

    @pl.when(ti == pl.num_programs(1) - 1)
    def _():
        sfin_ref[...] = st_scr[...]


def _s5(x, g_mix, w_in, bd, cd, a_re, a_im, d_skip, w_glu, s0, tt):
    b, s, _ = x.shape
    n_state = N_SSM_BLOCKS * 2 * SSM_BLOCK_STATE
    x_spec = pl.BlockSpec((SUBLANES, tt, D_MODEL), lambda bi, ti: (bi, ti, 0))
    s_spec = pl.BlockSpec((None, SUBLANES, n_state), lambda bi, ti: (bi, 0, 0))
    rows = tt * SUBLANES
    return pl.pallas_call(
        functools.partial(_s5_body, tt=tt),
        grid=(b // SUBLANES, s // tt),
        in_specs=[x_spec, _resident((1, D_MODEL)), _resident((D_MODEL, D_MODEL)),
                  _resident(bd.shape), _resident(cd.shape), _resident(a_re.shape),
                  _resident(a_im.shape), _resident((1, D_MODEL)),
                  _resident((D_MODEL, 2 * D_MODEL)), s_spec],
        out_specs=[x_spec, s_spec],
        out_shape=[jax.ShapeDtypeStruct((b, s, D_MODEL), F32),
                   jax.ShapeDtypeStruct((b // SUBLANES, SUBLANES, n_state), F32)],
        scratch_shapes=[pltpu.VMEM((D_MODEL // LANES, rows, LANES), F32),
                        pltpu.VMEM((SUBLANES, n_state), F32),
                        pltpu.VMEM((rows, 2 * SSM_BLOCK_STATE), F32),
                        pltpu.VMEM((rows, D_MODEL), F32)],
        compiler_params=pltpu.CompilerParams(
            dimension_semantics=("arbitrary", "arbitrary"), vmem_limit_bytes=VMEM_LIMIT),
        name="s5_layer",
    )(x, g_mix, w_in, bd, cd, a_re, a_im, d_skip, w_glu, s0)


def _s5_tables(lam_re, lam_im, log_dt, b_re, b_im, c_re, c_im):
    dt = jnp.exp(log_dt)[:, None]
    mag = jnp.exp(lam_re * dt)
    a_re = mag * jnp.cos(lam_im * dt)
    a_im = mag * jnp.sin(lam_im * dt)
    den = lam_re * lam_re + lam_im * lam_im
    zr = ((a_re - 1.0) * lam_re + a_im * lam_im) / den
    zi = (a_im * lam_re - (a_re - 1.0) * lam_im) / den
    bbar_re = zr[..., None] * b_re - zi[..., None] * b_im
    bbar_im = zr[..., None] * b_im + zi[..., None] * b_re
    eye = jnp.eye(SSM_GROUPS_PER_BLOCK, dtype=F32)

    def pack_in(m):
        m = m.reshape(N_SSM_BLOCKS, SSM_GROUPS_PER_BLOCK, STATE_DIM, SSM_GROUP)
        full = jnp.einsum("kgpc,gh->kgchp", m, eye)
        return full.reshape(N_SSM_BLOCKS, SSM_BLOCK_IN, SSM_BLOCK_STATE)

    def pack_out(m):
        m = m.reshape(N_SSM_BLOCKS, SSM_GROUPS_PER_BLOCK, SSM_GROUP, STATE_DIM)
        full = jnp.einsum("kgcp,gh->kgphc", m, eye)
        return full.reshape(N_SSM_BLOCKS, SSM_BLOCK_STATE, SSM_BLOCK_IN)

    bd = jnp.concatenate([pack_in(bbar_re), pack_in(bbar_im)], axis=2).astype(BF16)
    cd = jnp.concatenate([pack_out(c_re), pack_out(-c_im)], axis=1).astype(BF16)
    shape = (N_SSM_BLOCKS, 1, SSM_BLOCK_STATE)
    return bd, cd, a_re.reshape(shape), a_im.reshape(shape)


def _pack_state(s_re, s_im):
    n = s_re.shape[0]
    packed = jnp.concatenate([s_re.reshape(n, N_SSM_BLOCKS, SSM_BLOCK_STATE),
                              s_im.reshape(n, N_SSM_BLOCKS, SSM_BLOCK_STATE)], axis=2)
    return packed.reshape(n // SUBLANES, SUBLANES, -1)


def _unpack_state(packed, n):
    s = packed.reshape(n, N_SSM_BLOCKS, 2, SSM_BLOCK_STATE)
    return (s[:, :, 0].reshape(n, N_GROUPS, STATE_DIM), s[:, :, 1].reshape(n, N_GROUPS, STATE_DIM))


def _attention_layer(x_p, x_s, w, cache_kt, cache_vt, page_table, *, tm_p, tm_s, tq, n_slabs, pps):
    b, s, _ = x_p.shape
    db, ds, _ = x_s.shape
    q_p, kt, vt, ktb, vtb = _qkv_t(x_p.reshape(b * s, D_MODEL), w["norm_mix"][0], w["wq"], w["wkt"], w["wvt"],
                                   w["q_gain"], w["k_gain_col"], s, tm_p)
    q_s, k_s, v_s = _qkv(x_s.reshape(db * ds, D_MODEL), w["norm_mix"][0], w["wq"], w["wk"], w["wv"],
                         w["q_gain"], w["k_gain"], tm_s)
    shape_s = (db, ds, D_MODEL)
    o_p, o_s = _attention(q_p.reshape(b, s, D_MODEL), ktb, vtb, w["bias"] * LOG2_E, -w["tri"],
                          q_s.reshape(shape_s), k_s.reshape(shape_s), v_s.reshape(shape_s), w["bias_rows"],
                          w["tri"], cache_kt, cache_vt, page_table, tq, n_slabs, pps)
    feat_major = lambda a: jnp.transpose(a.reshape(b, N_HEADS, HEAD_DIM, s), (0, 3, 1, 2))
    heads_s = (db, ds, N_HEADS, HEAD_DIM)
    return (o_p, feat_major(kt), feat_major(vt)), (o_s, k_s.reshape(heads_s), v_s.reshape(heads_s))


def _after_attention(x, o, w, tables, *, tm, tt, ssm_prev=None, conv_prev=None):
    b, s, _ = x.shape
    t = b * s
    x2d = x.reshape(t, D_MODEL)
    shape3 = (b, s, D_MODEL)
    prev0 = None if conv_prev is None else conv_prev[0]
    x2d, conv0 = _ffn(x2d, o.reshape(t, D_MODEL), w["w_o"], w["norm_ffn"][0], w["w_up"][0],
                      w["conv_w"][0], w["conv_b"][0], w["w_down"][0], s, tm, prev0)

    bd, cd, a_re, a_im = tables
    if ssm_prev is None:
        s0 = jnp.zeros((b // SUBLANES, SUBLANES, N_SSM_BLOCKS * 2 * SSM_BLOCK_STATE), F32)
    else:
        s0 = _pack_state(*ssm_prev)
    x3, s_fin = _s5(x2d.reshape(shape3), w["norm_mix"][1], w["w_in"], bd, cd, a_re, a_im,
                    w["d_skip"], w["w_glu"], s0, tt)
    s_re, s_im = _unpack_state(s_fin, b)
    prev1 = None if conv_prev is None else conv_prev[1]
    y2d, conv1 = _ffn(x3.reshape(t, D_MODEL), None, None, w["norm_ffn"][1], w["w_up"][1],
                      w["conv_w"][1], w["conv_b"][1], w["w_down"][1], s, tm, prev1)

    def conv_state(c):
        if conv_prev is None:
            per_seq = c.reshape(b, s // tm, SUBLANES, D_FF)[:, -1]
        else:
            per_seq = c.reshape(b, s, D_FF)
        return per_seq[:, -2:]

    return (y2d.reshape(shape3), s_re[None], s_im[None],
            jnp.stack([conv_state(conv0), conv_state(conv1)]))


def _short_seq_prev_rows(prev, seq_len):
    bsz = prev.shape[0]
    zeros = jnp.zeros((bsz, seq_len - 2, D_FF), F32)
    p2 = jnp.concatenate([prev, zeros], axis=1)
    p1 = jnp.concatenate([prev[:, 1:2], jnp.zeros((bsz, seq_len - 1, D_FF), F32)], axis=1)
    return p1.reshape(bsz * seq_len, D_FF), p2.reshape(bsz * seq_len, D_FF)


def kernel(x_prompt, x_sample, cache_k, cache_v, state_ssm_re, state_ssm_im, state_ffn_conv, page_table,
           norm_mix, norm_ffn, attn_w_qkv, attn_q_gain, attn_k_gain, attn_logit_bias, attn_w_o,
           ssm_w_in, ssm_lambda_re, ssm_lambda_im, ssm_log_dt, ssm_b_re, ssm_b_im, ssm_c_re, ssm_c_im,
           ssm_d, ssm_w_glu, ffn_w_up, ffn_conv_w, ffn_conv_b, ffn_w_down):
    dec_batch, dec_seq, _ = x_sample.shape
    w = {
        "norm_mix": norm_mix[:, None, :], "norm_ffn": norm_ffn[:, None, :],
        "wq": attn_w_qkv[0, :, 0:D_MODEL].astype(BF16),
        "wk": attn_w_qkv[0, :, D_MODEL:2 * D_MODEL].astype(BF16),
        "wv": attn_w_qkv[0, :, 2 * D_MODEL:3 * D_MODEL].astype(BF16),
        "wkt": attn_w_qkv[0, :, D_MODEL:2 * D_MODEL].T.astype(BF16),
        "wvt": attn_w_qkv[0, :, 2 * D_MODEL:3 * D_MODEL].T.astype(BF16),
        "q_gain": jnp.tile(attn_q_gain[0], N_HEADS)[None], "k_gain": jnp.tile(attn_k_gain[0], N_HEADS)[None],
        "k_gain_col": attn_k_gain[0][:, None],
        "bias": attn_logit_bias[0],
        "bias_rows": jnp.repeat(attn_logit_bias[0], dec_seq)[:, None],
        "tri": jnp.tril(jnp.ones((MXU_DIM, MXU_DIM), BF16)),
        "w_o": attn_w_o[0].astype(BF16),
        "w_in": ssm_w_in[0].astype(BF16), "d_skip": ssm_d[0][None], "w_glu": ssm_w_glu[0].astype(BF16),
        "w_up": ffn_w_up.astype(BF16), "conv_w": ffn_conv_w, "conv_b": ffn_conv_b[:, None, :],
        "w_down": ffn_w_down.astype(BF16),
    }
    tables = _s5_tables(ssm_lambda_re[0], ssm_lambda_im[0], ssm_log_dt[0], ssm_b_re[0], ssm_b_im[0],
                        ssm_c_re[0], ssm_c_im[0])

    n_pool = cache_k.shape[1]
    to_pages = lambda c: jnp.transpose(c[0], (0, 2, 3, 1)).reshape(n_pool, D_MODEL, PAGE_SIZE)
    tm_s = dec_batch * dec_seq
    (o_p, k_p, v_p), (o_s, k_s, v_s) = _attention_layer(
        x_prompt, x_sample, w, to_pages(cache_k), to_pages(cache_v), page_table,
        tm_p=512, tm_s=tm_s, tq=256, n_slabs=2, pps=8)

    y_p, sr_p, si_p, conv_p = _after_attention(x_prompt, o_p, w, tables, tm=512, tt=64)
    conv_prev = [_short_seq_prev_rows(state_ffn_conv[i], dec_seq) for i in range(state_ffn_conv.shape[0])]
    y_s, sr_s, si_s, conv_s = _after_attention(
        x_sample, o_s, w, tables, tm=tm_s, tt=dec_seq,
        ssm_prev=(state_ssm_re[0], state_ssm_im[0]), conv_prev=conv_prev)
    k_p, v_p, k_s, v_s = k_p[None], v_p[None], k_s[None], v_s[None]

    return (y_p, y_s, k_p, v_p, k_s, v_s, sr_p, si_p, sr_s, si_s, conv_p, conv_s)
```

```python
import functools
import math

import jax
import jax.numpy as jnp
from jax import lax
from jax.experimental import pallas as pl
from jax.experimental.pallas import tpu as pltpu

F32 = jnp.float32
BF16 = jnp.bfloat16

D_MODEL = 1024
N_HEADS = 16
HEAD_DIM = 64
PAGE_SIZE = 128
SSM_GROUP = 16
N_GROUPS = D_MODEL // SSM_GROUP
STATE_DIM = 64
D_FF = 2816
NORM_EPS = 1e-6

LANES = 128
SUBLANES = 8
MXU_DIM = 256
VMEM_LIMIT = 56 * 1024 * 1024

HEADS_PER_SLAB = LANES // HEAD_DIM
SSM_BLOCK_IN = MXU_DIM
SSM_GROUPS_PER_BLOCK = SSM_BLOCK_IN // SSM_GROUP
SSM_BLOCK_STATE = SSM_GROUPS_PER_BLOCK * STATE_DIM
N_SSM_BLOCKS = D_MODEL // SSM_BLOCK_IN
SSM_SCAN_LANES = 512
FF_CHUNK = MXU_DIM
LOG2_E = math.log2(math.e)
MASKED_LOGIT = -1e30


def _dot(a, b):
    return jnp.dot(a, b, preferred_element_type=F32)


def _dot_nt(a, b):
    return lax.dot_general(a, b, (((1,), (1,)), ((), ())), preferred_element_type=F32)


def _rms(x, gain):
    return x * lax.rsqrt(jnp.mean(x * x, axis=-1, keepdims=True) + NORM_EPS) * gain


def _resident(shape):
    nd = len(shape)
    return pl.BlockSpec(shape, lambda *_: (0,) * nd, pipeline_mode=pl.Buffered(1))


def _neg_softplus(z):
    return -(jnp.maximum(z, 0.0) + jnp.log(1.0 + jnp.exp(-jnp.abs(z))))


def _head_norm_rows(t, gain):
    r = lax.broadcasted_iota(jnp.int32, (MXU_DIM, MXU_DIM), 0) // HEAD_DIM
    c = lax.broadcasted_iota(jnp.int32, (MXU_DIM, MXU_DIM), 1) // HEAD_DIM
    seg = (r == c).astype(BF16)
    sq = (t * t).astype(BF16)
    ms = jnp.concatenate(
        [_dot(sq[:, j * MXU_DIM:(j + 1) * MXU_DIM], seg) for j in range(D_MODEL // MXU_DIM)],
        axis=1) * (1.0 / HEAD_DIM)
    return t * lax.rsqrt(ms + NORM_EPS) * gain


def _qkv_body(x_ref, gm_ref, wq_ref, wk_ref, wv_ref, qg_ref, kg_ref, q_ref, k_ref, v_ref):
    h = _rms(x_ref[...], gm_ref[...]).astype(BF16)
    q = _head_norm_rows(_dot(h, wq_ref[...]), qg_ref[...])
    q_ref[...] = (q * (HEAD_DIM ** -0.5)).astype(BF16)
    k_ref[...] = _head_norm_rows(_dot(h, wk_ref[...]), kg_ref[...])
    v_ref[...] = _dot(h, wv_ref[...])


def _qkv(x2d, g_mix, wq, wk, wv, q_gain, k_gain, tm):
    t = x2d.shape[0]
    tok = pl.BlockSpec((tm, D_MODEL), lambda i: (i, 0))
    wspec = _resident((D_MODEL, D_MODEL))
    return pl.pallas_call(
        _qkv_body,
        grid=(t // tm,),
        in_specs=[tok, _resident((1, D_MODEL)), wspec, wspec, wspec,
                  _resident((1, D_MODEL)), _resident((1, D_MODEL))],
        out_specs=[tok, tok, tok],
        out_shape=[jax.ShapeDtypeStruct((t, D_MODEL), BF16),
                   jax.ShapeDtypeStruct((t, D_MODEL), F32),
                   jax.ShapeDtypeStruct((t, D_MODEL), F32)],
        compiler_params=pltpu.CompilerParams(
            dimension_semantics=("arbitrary",), vmem_limit_bytes=VMEM_LIMIT),
        name="qkv_proj",
    )(x2d, g_mix, wq, wk, wv, q_gain, k_gain)


def _qkv_t_body(x_ref, gm_ref, wq_ref, wkt_ref, wvt_ref, qg_ref, kgt_ref,
                q_ref, kt_ref, vt_ref, ktb_ref, vtb_ref):
    tm = x_ref.shape[0]
    h = _rms(x_ref[...], gm_ref[...]).astype(BF16)
    q = _head_norm_rows(_dot(h, wq_ref[...]), qg_ref[...])
    q_ref[...] = (q * (HEAD_DIM ** -0.5 * LOG2_E)).astype(BF16)
    k3 = _dot_nt(wkt_ref[...], h).reshape(N_HEADS, HEAD_DIM, tm)
    ms = jnp.mean(k3 * k3, axis=1, keepdims=True)
    kt = (k3 * lax.rsqrt(ms + NORM_EPS) * kgt_ref[...][None]).reshape(D_MODEL, tm)
    kt_ref[...] = kt
    ktb_ref[...] = kt.astype(BF16)
    vt = _dot_nt(wvt_ref[...], h)
    vt_ref[...] = vt
    vtb_ref[...] = vt.astype(BF16)


def _qkv_t(x2d, g_mix, wq, wkt, wvt, q_gain, k_gain_col, seq_len, tm):
    t = x2d.shape[0]
    tiles_per_seq = seq_len // tm
    tok = pl.BlockSpec((tm, D_MODEL), lambda i: (i, 0))
    feat = pl.BlockSpec((None, D_MODEL, tm), lambda i: (i // tiles_per_seq, 0, i % tiles_per_seq))
    wspec = _resident((D_MODEL, D_MODEL))
    shape_t = (t // seq_len, D_MODEL, seq_len)
    return pl.pallas_call(
        _qkv_t_body,
        grid=(t // tm,),
        in_specs=[tok, _resident((1, D_MODEL)), wspec, wspec, wspec,
                  _resident((1, D_MODEL)), _resident((HEAD_DIM, 1))],
        out_specs=[tok, feat, feat, feat, feat],
        out_shape=[jax.ShapeDtypeStruct((t, D_MODEL), BF16),
                   jax.ShapeDtypeStruct(shape_t, F32),
                   jax.ShapeDtypeStruct(shape_t, F32),
                   jax.ShapeDtypeStruct(shape_t, BF16),
                   jax.ShapeDtypeStruct(shape_t, BF16)],
        compiler_params=pltpu.CompilerParams(
            dimension_semantics=("arbitrary",), vmem_limit_bytes=VMEM_LIMIT),
        name="qkv_proj_t",
    )(x2d, g_mix, wq, wkt, wvt, q_gain, k_gain_col)


def _attn_prompt_body(bias_ref, tri_ref, q_ref, kt_ref, vt_ref, o_ref, acc_ref, z_scr, i_scr,
                      *, tq, n_slabs):
    grp = pl.program_id(1)
    i = pl.program_id(2)
    n_heads = n_slabs * HEADS_PER_SLAB
    low = lax.broadcasted_iota(jnp.int32, (tq, LANES), 1) < HEAD_DIM
    q_heads = []
    for s in range(n_slabs):
        q = q_ref[:, s * LANES:(s + 1) * LANES]
        q_heads += [jnp.where(low, q, jnp.zeros_like(q)), jnp.where(low, jnp.zeros_like(q), q)]
    bias = [bias_ref[grp * n_heads + h] for h in range(n_heads)]
    row = lax.broadcasted_iota(jnp.int32, (tq, tq), 0)
    col = lax.broadcasted_iota(jnp.int32, (tq, tq), 1)
    causal = col < row

    acc_ref[...] = jnp.zeros_like(acc_ref)

    def rows_of(h):
        return slice((h // HEADS_PER_SLAB) * LANES, (h // HEADS_PER_SLAB + 1) * LANES)

    def stage_a(j, diag):
        keys = pl.ds(pl.multiple_of(j * tq, tq), tq)
        totals = []
        for h in range(n_heads):
            z = _dot(q_heads[h], kt_ref[rows_of(h), keys]) + bias[h]
            if diag:
                z = jnp.where(causal, z, MASKED_LOGIT)
            sp = jnp.maximum(z, 0.0) + jnp.log2(1.0 + jnp.exp2(-jnp.abs(z)))
            incl = _dot(sp.astype(BF16), tri_ref[...])
            z_scr[h] = z
            i_scr[h] = incl
            totals.append(incl[:, 0:1])
        return tuple(totals)

    def stage_b(j, carry):
        keys = pl.ds(pl.multiple_of(j * tq, tq), tq)
        for h in range(n_heads):
            w = jnp.exp2(z_scr[h] + i_scr[h] + carry[h])
            acc_ref[h] += _dot_nt(w.astype(BF16), vt_ref[rows_of(h), keys])

    def step(jj, state):
        carry, totals = state
        j = i - 1 - jj
        stage_b(j + 1, carry)
        carry = tuple(c + t for c, t in zip(carry, totals))
        return carry, stage_a(j, False)

    zero = jnp.zeros((tq, 1), F32)
    carry, _ = lax.fori_loop(0, i, step, ((zero,) * n_heads, stage_a(i, True)))
    stage_b(0, carry)
    for s in range(n_slabs):
        o_ref[:, s * LANES:(s + 1) * LANES] = jnp.where(
            low, acc_ref[HEADS_PER_SLAB * s], acc_ref[HEADS_PER_SLAB * s + 1]).astype(BF16)


def _attn_prompt(q, kt, vt, bias, tri, tq, n_slabs):
    b, s, _ = q.shape
    width = n_slabs * LANES
    kv_spec = pl.BlockSpec((None, width, s), lambda bi, gi, qi: (bi, gi, 0))
    q_spec = pl.BlockSpec((None, tq, width), lambda bi, gi, qi: (bi, qi, gi))
    return pl.pallas_call(
        functools.partial(_attn_prompt_body, tq=tq, n_slabs=n_slabs),
        grid=(b, D_MODEL // width, s // tq),
        in_specs=[pl.BlockSpec(memory_space=pltpu.SMEM),
                  pl.BlockSpec((tq, tq), lambda bi, gi, qi: (0, 0)), q_spec, kv_spec, kv_spec],
        out_specs=q_spec,
        out_shape=jax.ShapeDtypeStruct((b, s, D_MODEL), BF16),
        scratch_shapes=[pltpu.VMEM((n_slabs * HEADS_PER_SLAB, tq, LANES), F32),
                        pltpu.VMEM((n_slabs * HEADS_PER_SLAB, tq, tq), F32),
                        pltpu.VMEM((n_slabs * HEADS_PER_SLAB, tq, tq), F32)],
        compiler_params=pltpu.CompilerParams(
            dimension_semantics=("arbitrary", "arbitrary", "arbitrary"),
            vmem_limit_bytes=VMEM_LIMIT),
        name="attn_prompt",
    )(bias, tri, q, kt, vt)


def _attn_sample_body(pt_ref, q_ref, kn_ref, vn_ref, bias_ref, tri_ref, *rest, n_q, pps):
    del pt_ref
    kt_refs, vt_refs = rest[:pps], rest[pps:2 * pps]
    o_ref, qbd_ref, kt_scr, vt_scr, w_scr, acc_ref, carry_ref = rest[2 * pps:]
    step = pl.program_id(1)
    n_rows = N_HEADS * n_q
    window = pps * PAGE_SIZE

    @pl.when(step == 0)
    def _():
        q_rep = jnp.concatenate([q_ref[...].astype(F32)] * N_HEADS, axis=0)
        rh = lax.broadcasted_iota(jnp.int32, (n_rows, D_MODEL), 0) // n_q
        ch = lax.broadcasted_iota(jnp.int32, (n_rows, D_MODEL), 1) // HEAD_DIM
        qbd = jnp.where(rh == ch, q_rep, 0.0).astype(BF16)
        qbd_ref[...] = qbd
        pad = jnp.zeros((PAGE_SIZE - n_q, D_MODEL), F32)
        kn = jnp.concatenate([kn_ref[...], pad], axis=0).astype(BF16)
        vn = jnp.concatenate([vn_ref[...], pad], axis=0).astype(BF16)
        z = _dot_nt(qbd, kn) + bias_ref[...]
        key = lax.broadcasted_iota(jnp.int32, (n_rows, PAGE_SIZE), 1)
        qi = lax.broadcasted_iota(jnp.int32, (n_rows, PAGE_SIZE), 0) % n_q
        visible = key < qi
        lk = jnp.where(visible, _neg_softplus(z), 0.0)
        incl = _dot(lk.astype(BF16), tri_ref[0:PAGE_SIZE, 0:PAGE_SIZE])
        w = jnp.where(visible, jnp.exp(z + incl), 0.0)
        acc_ref[...] = _dot(w.astype(BF16), vn)
        carry_ref[...] = jnp.broadcast_to(incl[:, 0:1], carry_ref.shape)

    for i in range(pps):
        lanes = slice(i * PAGE_SIZE, (i + 1) * PAGE_SIZE)
        kt_scr[:, lanes] = kt_refs[i][...].astype(BF16)
        vt_scr[:, lanes] = vt_refs[i][...].astype(BF16)
    z_all = _dot(qbd_ref[...], kt_scr[...]) + bias_ref[...]
    carry = carry_ref[:, 0:1]
    for ch in reversed(range(window // MXU_DIM)):
        lanes = slice(ch * MXU_DIM, (ch + 1) * MXU_DIM)
        z = z_all[:, lanes]
        incl = _dot(_neg_softplus(z).astype(BF16), tri_ref[...])
        w_scr[:, lanes] = jnp.exp(z + incl + carry).astype(BF16)
        carry = carry + incl[:, 0:1]
    acc_ref[...] += _dot_nt(w_scr[...], vt_scr[...])
    carry_ref[...] = jnp.broadcast_to(carry, carry_ref.shape)

    @pl.when(step == pl.num_programs(1) - 1)
    def _():
        rh = lax.broadcasted_iota(jnp.int32, (n_rows, D_MODEL), 0) // n_q
        ch = lax.broadcasted_iota(jnp.int32, (n_rows, D_MODEL), 1) // HEAD_DIM
        own = jnp.where(rh == ch, acc_ref[...], 0.0)
        out = own[0:n_q, :]
        for h in range(1, N_HEADS):
            out = out + own[h * n_q:(h + 1) * n_q, :]
        o_ref[...] = out.astype(BF16)


def _attn_sample(q, k_new, v_new, bias_rows, tri, cache_kt, cache_vt, page_table, pps):
    db, n_q, _ = q.shape
    n_pages = page_table.shape[1]
    n_rows = N_HEADS * n_q
    window = pps * PAGE_SIZE

    def page_spec(i):
        def idx(bi, si, pt):
            return (pt[bi, n_pages - pps * (si + 1) + i], 0, 0)
        return pl.BlockSpec((None, D_MODEL, PAGE_SIZE), idx)

    per_seq = lambda bi, si, pt: (bi, 0, 0)
    const = lambda bi, si, pt: (0, 0)
    pages = [page_spec(i) for i in range(pps)]
    grid_spec = pltpu.PrefetchScalarGridSpec(
        num_scalar_prefetch=1,
        grid=(db, n_pages // pps),
        in_specs=[pl.BlockSpec((None, n_q, D_MODEL), per_seq),
                  pl.BlockSpec((None, n_q, D_MODEL), per_seq),
                  pl.BlockSpec((None, n_q, D_MODEL), per_seq),
                  pl.BlockSpec((n_rows, 1), const),
                  pl.BlockSpec((MXU_DIM, MXU_DIM), const)] + pages + pages,
        out_specs=pl.BlockSpec((None, n_q, D_MODEL), per_seq),
        scratch_shapes=[pltpu.VMEM((n_rows, D_MODEL), BF16),
                        pltpu.VMEM((D_MODEL, window), BF16),
                        pltpu.VMEM((D_MODEL, window), BF16),
                        pltpu.VMEM((n_rows, window), BF16),
                        pltpu.VMEM((n_rows, D_MODEL), F32),
                        pltpu.VMEM((n_rows, LANES), F32)])
    return pl.pallas_call(
        functools.partial(_attn_sample_body, n_q=n_q, pps=pps),
        grid_spec=grid_spec,
        out_shape=jax.ShapeDtypeStruct((db, n_q, D_MODEL), BF16),
        compiler_params=pltpu.CompilerParams(
            dimension_semantics=("arbitrary", "arbitrary"), vmem_limit_bytes=VMEM_LIMIT),
        name="attn_sample",
    )(page_table, q, k_new, v_new, bias_rows, tri, *([cache_kt] * pps), *([cache_vt] * pps))


def _ffn_body(*refs, with_wo, short_seq, tiles_per_seq, seq_len):
    refs = list(refs)
    x_ref = refs.pop(0)
    if with_wo:
        o_ref, wo_ref = refs.pop(0), refs.pop(0)
    g_ref, wup_ref, cw_ref, cb_ref, wdn_ref = [refs.pop(0) for _ in range(5)]
    if short_seq:
        p1_ref, p2_ref = refs.pop(0), refs.pop(0)
    out_ref, conv_ref, a_scr, g_scr = refs
    tm = x_ref.shape[0]

    x1 = x_ref[...]
    if with_wo:
        x1 = x1 + _dot(o_ref[...], wo_ref[...])
    h = _rms(x1, g_ref[...]).astype(BF16)

    if short_seq:
        a_scr[0:SUBLANES, :] = jnp.zeros((SUBLANES, D_FF), F32)
    else:
        @pl.when(pl.program_id(0) % tiles_per_seq == 0)
        def _():
            a_scr[0:SUBLANES, :] = jnp.zeros((SUBLANES, D_FF), F32)

    a_scr[SUBLANES:SUBLANES + tm, :] = _dot(h, wup_ref[:, 0:D_FF])

    row = lax.broadcasted_iota(jnp.int32, (tm, FF_CHUNK), 0)
    for c in range(D_FF // FF_CHUNK):
        cs = slice(c * FF_CHUNK, (c + 1) * FF_CHUNK)
        a0 = a_scr[SUBLANES:SUBLANES + tm, cs]
        a1 = a_scr[SUBLANES - 1:SUBLANES - 1 + tm, cs]
        a2 = a_scr[SUBLANES - 2:SUBLANES - 2 + tm, cs]
        if short_seq:
            pos = row % seq_len
            a1 = jnp.where(pos < 1, p1_ref[:, cs], a1)
            a2 = jnp.where(pos < 2, p2_ref[:, cs], a2)
        b = _dot(h, wup_ref[:, D_FF + c * FF_CHUNK:D_FF + (c + 1) * FF_CHUNK])
        cc = cb_ref[:, cs] + cw_ref[0:1, cs] * a2 + cw_ref[1:2, cs] * a1 + cw_ref[2:3, cs] * a0
        g_scr[:, cs] = (cc * jax.nn.sigmoid(cc) * b).astype(BF16)

    tail = a_scr[tm:tm + SUBLANES, :]
    if short_seq:
        conv_ref[...] = a_scr[SUBLANES:SUBLANES + tm, :]
    else:
        conv_ref[...] = tail
        a_scr[0:SUBLANES, :] = tail
    out_ref[...] = x1 + _dot(g_scr[...], wdn_ref[...])


def _ffn(x2d, o2d, w_o, g_ffn, w_up, conv_w, conv_b, w_down, seq_len, tm, prev_rows=None):
    t = x2d.shape[0]
    with_wo = o2d is not None
    short_seq = prev_rows is not None
    n_tiles = t // tm
    row = lambda i: (i, 0)
    tok = pl.BlockSpec((tm, D_MODEL), row)
    args, specs = [x2d], [tok]
    if with_wo:
        args += [o2d, w_o]
        specs += [tok, _resident((D_MODEL, D_MODEL))]
    args += [g_ffn, w_up, conv_w, conv_b, w_down]
    specs += [_resident((1, D_MODEL)), _resident((D_MODEL, 2 * D_FF)), _resident((3, D_FF)),
              _resident((1, D_FF)), _resident((D_FF, D_MODEL))]
    if short_seq:
        args += list(prev_rows)
        specs += [pl.BlockSpec((tm, D_FF), row)] * 2
        conv_shape, conv_spec = (t, D_FF), pl.BlockSpec((tm, D_FF), row)
        tiles_per_seq = 1
    else:
        conv_shape = (n_tiles, SUBLANES, D_FF)
        conv_spec = pl.BlockSpec((None, SUBLANES, D_FF), lambda i: (i, 0, 0))
        tiles_per_seq = seq_len // tm
    return pl.pallas_call(
        functools.partial(_ffn_body, with_wo=with_wo, short_seq=short_seq,
                          tiles_per_seq=tiles_per_seq, seq_len=seq_len),
        grid=(n_tiles,),
        in_specs=specs,
        out_specs=[tok, conv_spec],
        out_shape=[jax.ShapeDtypeStruct((t, D_MODEL), F32),
                   jax.ShapeDtypeStruct(conv_shape, F32)],
        scratch_shapes=[pltpu.VMEM((tm + SUBLANES, D_FF), F32), pltpu.VMEM((tm, D_FF), BF16)],
        compiler_params=pltpu.CompilerParams(
            dimension_semantics=("arbitrary",), vmem_limit_bytes=VMEM_LIMIT),
        name="conv_ffn",
    )(*args)


def _s5_body(x_ref, g_ref, win_ref, bd_ref, cd_ref, are_ref, aim_ref, dskip_ref, wglu_ref, s0_ref,
             xo_ref, sfin_ref, xt_scr, st_scr, bu_scr, y_scr, *, tt):
    ti = pl.program_id(1)
    n_seq = SUBLANES

    @pl.when(ti == 0)
    def _():
        st_scr[...] = s0_ref[...]

    n_slabs = D_MODEL // LANES
    for b in range(n_seq):
        for j in range(n_slabs):
            xt_scr[j, pl.ds(b, tt, stride=n_seq), :] = x_ref[b, :, j * LANES:(j + 1) * LANES]
    xt = jnp.concatenate([xt_scr[j] for j in range(n_slabs)], axis=1)
    u = _dot(_rms(xt, g_ref[...]).astype(BF16), win_ref[...])
    ub = u.astype(BF16)

    for k in range(N_SSM_BLOCKS):
        bu_scr[...] = _dot(ub[:, k * SSM_BLOCK_IN:(k + 1) * SSM_BLOCK_IN], bd_ref[k])
        for lc in range(SSM_BLOCK_STATE // SSM_SCAN_LANES):
            re_l = slice(lc * SSM_SCAN_LANES, (lc + 1) * SSM_SCAN_LANES)
            im_l = slice(SSM_BLOCK_STATE + lc * SSM_SCAN_LANES,
                         SSM_BLOCK_STATE + (lc + 1) * SSM_SCAN_LANES)
            base = k * 2 * SSM_BLOCK_STATE
            st_re = slice(base + re_l.start, base + re_l.stop)
            st_im = slice(base + im_l.start, base + im_l.stop)
            ar = jnp.broadcast_to(are_ref[k, :, re_l], (n_seq, SSM_SCAN_LANES))
            ai = jnp.broadcast_to(aim_ref[k, :, re_l], (n_seq, SSM_SCAN_LANES))

            def step(t, s, ar=ar, ai=ai, re_l=re_l, im_l=im_l):
                sr, si = s
                r0 = pl.multiple_of(t * n_seq, n_seq)
                nr = ar * sr - ai * si + bu_scr[pl.ds(r0, n_seq), re_l]
                ni = ar * si + ai * sr + bu_scr[pl.ds(r0, n_seq), im_l]
                bu_scr[pl.ds(r0, n_seq), re_l] = nr
                bu_scr[pl.ds(r0, n_seq), im_l] = ni
                return nr, ni

            sr, si = lax.fori_loop(0, tt, step, (st_scr[:, st_re], st_scr[:, st_im]), unroll=True)
            st_scr[:, st_re] = sr
            st_scr[:, st_im] = si
        y_scr[:, k * SSM_BLOCK_IN:(k + 1) * SSM_BLOCK_IN] = _dot(bu_scr[...].astype(BF16), cd_ref[k])

    y = y_scr[...] + dskip_ref[...] * u
    gg = _dot(jax.nn.gelu(y).astype(BF16), wglu_ref[...])
    x_new = xt + gg[:, 0:D_MODEL] * jax.nn.sigmoid(gg[:, D_MODEL:2 * D_MODEL])
    for j in range(n_slabs):
        xt_scr[j] = x_new[:, j * LANES:(j + 1) * LANES]
    for b in range(n_seq):
        for j in range(n_slabs):
            xo_ref[b, :, j * LANES:(j + 1) * LANES] = xt_scr[j, pl.ds(b, tt, stride=n_seq), :]

    @pl.when(ti == pl.num_programs(1) - 1)
    def _():
        sfin_ref[...] = st_scr[...]


def _s5(x, g_mix, w_in, bd, cd, a_re, a_im, d_skip, w_glu, s0, tt):
    b, s, _ = x.shape
    n_state = N_SSM_BLOCKS * 2 * SSM_BLOCK_STATE
    x_spec = pl.BlockSpec((SUBLANES, tt, D_MODEL), lambda bi, ti: (bi, ti, 0))
    s_spec = pl.BlockSpec((None, SUBLANES, n_state), lambda bi, ti: (bi, 0, 0))
    rows = tt * SUBLANES
    return pl.pallas_call(
        functools.partial(_s5_body, tt=tt),
        grid=(b // SUBLANES, s // tt),
        in_specs=[x_spec, _resident((1, D_MODEL)), _resident((D_MODEL, D_MODEL)),
                  _resident(bd.shape), _resident(cd.shape), _resident(a_re.shape),
                  _resident(a_im.shape), _resident((1, D_MODEL)),
                  _resident((D_MODEL, 2 * D_MODEL)), s_spec],
        out_specs=[x_spec, s_spec],
        out_shape=[jax.ShapeDtypeStruct((b, s, D_MODEL), F32),
                   jax.ShapeDtypeStruct((b // SUBLANES, SUBLANES, n_state), F32)],
        scratch_shapes=[pltpu.VMEM((D_MODEL // LANES, rows, LANES), F32),
                        pltpu.VMEM((SUBLANES, n_state), F32),
                        pltpu.VMEM((rows, 2 * SSM_BLOCK_STATE), F32),
                        pltpu.VMEM((rows, D_MODEL), F32)],
        compiler_params=pltpu.CompilerParams(
            dimension_semantics=("arbitrary", "arbitrary"), vmem_limit_bytes=VMEM_LIMIT),
        name="s5_layer",
    )(x, g_mix, w_in, bd, cd, a_re, a_im, d_skip, w_glu, s0)


def _s5_tables(lam_re, lam_im, log_dt, b_re, b_im, c_re, c_im):
    dt = jnp.exp(log_dt)[:, None]
    mag = jnp.exp(lam_re * dt)
    a_re = mag * jnp.cos(lam_im * dt)
    a_im = mag * jnp.sin(lam_im * dt)
    den = lam_re * lam_re + lam_im * lam_im
    zr = ((a_re - 1.0) * lam_re + a_im * lam_im) / den
    zi = (a_im * lam_re - (a_re - 1.0) * lam_im) / den
    bbar_re = zr[..., None] * b_re - zi[..., None] * b_im
    bbar_im = zr[..., None] * b_im + zi[..., None] * b_re
    eye = jnp.eye(SSM_GROUPS_PER_BLOCK, dtype=F32)

    def pack_in(m):
        m = m.reshape(N_SSM_BLOCKS, SSM_GROUPS_PER_BLOCK, STATE_DIM, SSM_GROUP)
        full = jnp.einsum("kgpc,gh->kgchp", m, eye)
        return full.reshape(N_SSM_BLOCKS, SSM_BLOCK_IN, SSM_BLOCK_STATE)

    def pack_out(m):
        m = m.reshape(N_SSM_BLOCKS, SSM_GROUPS_PER_BLOCK, SSM_GROUP, STATE_DIM)
        full = jnp.einsum("kgcp,gh->kgphc", m, eye)
        return full.reshape(N_SSM_BLOCKS, SSM_BLOCK_STATE, SSM_BLOCK_IN)

    bd = jnp.concatenate([pack_in(bbar_re), pack_in(bbar_im)], axis=2).astype(BF16)
    cd = jnp.concatenate([pack_out(c_re), pack_out(-c_im)], axis=1).astype(BF16)
    shape = (N_SSM_BLOCKS, 1, SSM_BLOCK_STATE)
    return bd, cd, a_re.reshape(shape), a_im.reshape(shape)


def _pack_state(s_re, s_im):
    n = s_re.shape[0]
    packed = jnp.concatenate([s_re.reshape(n, N_SSM_BLOCKS, SSM_BLOCK_STATE),
                              s_im.reshape(n, N_SSM_BLOCKS, SSM_BLOCK_STATE)], axis=2)
    return packed.reshape(n // SUBLANES, SUBLANES, -1)


def _unpack_state(packed, n):
    s = packed.reshape(n, N_SSM_BLOCKS, 2, SSM_BLOCK_STATE)
    return (s[:, :, 0].reshape(n, N_GROUPS, STATE_DIM), s[:, :, 1].reshape(n, N_GROUPS, STATE_DIM))


def _trunk(x, w, tables, *, tm, tq, tt, past=None, ssm_prev=None, conv_prev=None):
    b, s, _ = x.shape
    t = b * s
    x2d = x.reshape(t, D_MODEL)

    shape3 = (b, s, D_MODEL)
    heads = (b, s, N_HEADS, HEAD_DIM)
    if past is None:
        q, kt, vt, ktb, vtb = _qkv_t(x2d, w["norm_mix"][0], w["wq"], w["wkt"], w["wvt"], w["q_gain"],
                                     w["k_gain_col"], s, tm)
        o = _attn_prompt(q.reshape(shape3), ktb, vtb, w["bias"] * LOG2_E, -w["tri"], tq, 8)
        feat_major = lambda a: jnp.transpose(a.reshape(b, N_HEADS, HEAD_DIM, s), (0, 3, 1, 2))
        k, v = feat_major(kt), feat_major(vt)
    else:
        q, k, v = _qkv(x2d, w["norm_mix"][0], w["wq"], w["wk"], w["wv"], w["q_gain"], w["k_gain"], tm)
        cache_kt, cache_vt, page_table, pps = past
        o = _attn_sample(q.reshape(shape3), k.reshape(shape3), v.reshape(shape3), w["bias_rows"],
                         w["tri"], cache_kt, cache_vt, page_table, pps)
        k, v = k.reshape(heads), v.reshape(heads)
    prev0 = None if conv_prev is None else conv_prev[0]
    x2d, conv0 = _ffn(x2d, o.reshape(t, D_MODEL), w["w_o"], w["norm_ffn"][0], w["w_up"][0],
                      w["conv_w"][0], w["conv_b"][0], w["w_down"][0], s, tm, prev0)

    bd, cd, a_re, a_im = tables
    if ssm_prev is None:
        s0 = jnp.zeros((b // SUBLANES, SUBLANES, N_SSM_BLOCKS * 2 * SSM_BLOCK_STATE), F32)
    else:
        s0 = _pack_state(*ssm_prev)
    x3, s_fin = _s5(x2d.reshape(shape3), w["norm_mix"][1], w["w_in"], bd, cd, a_re, a_im,
                    w["d_skip"], w["w_glu"], s0, tt)
    s_re, s_im = _unpack_state(s_fin, b)
    prev1 = None if conv_prev is None else conv_prev[1]
    y2d, conv1 = _ffn(x3.reshape(t, D_MODEL), None, None, w["norm_ffn"][1], w["w_up"][1],
                      w["conv_w"][1], w["conv_b"][1], w["w_down"][1], s, tm, prev1)

    def conv_state(c):
        if conv_prev is None:
            per_seq = c.reshape(b, s // tm, SUBLANES, D_FF)[:, -1]
        else:
            per_seq = c.reshape(b, s, D_FF)
        return per_seq[:, -2:]

    return (y2d.reshape(shape3), k[None], v[None], s_re[None], s_im[None],
            jnp.stack([conv_state(conv0), conv_state(conv1)]))


def _short_seq_prev_rows(prev, seq_len):
    bsz = prev.shape[0]
    zeros = jnp.zeros((bsz, seq_len - 2, D_FF), F32)
    p2 = jnp.concatenate([prev, zeros], axis=1)
    p1 = jnp.concatenate([prev[:, 1:2], jnp.zeros((bsz, seq_len - 1, D_FF), F32)], axis=1)
    return p1.reshape(bsz * seq_len, D_FF), p2.reshape(bsz * seq_len, D_FF)


def kernel(x_prompt, x_sample, cache_k, cache_v, state_ssm_re, state_ssm_im, state_ffn_conv, page_table,
           norm_mix, norm_ffn, attn_w_qkv, attn_q_gain, attn_k_gain, attn_logit_bias, attn_w_o,
           ssm_w_in, ssm_lambda_re, ssm_lambda_im, ssm_log_dt, ssm_b_re, ssm_b_im, ssm_c_re, ssm_c_im,
           ssm_d, ssm_w_glu, ffn_w_up, ffn_conv_w, ffn_conv_b, ffn_w_down):
    dec_batch, dec_seq, _ = x_sample.shape
    w = {
        "norm_mix": norm_mix[:, None, :], "norm_ffn": norm_ffn[:, None, :],
        "wq": attn_w_qkv[0, :, 0:D_MODEL].astype(BF16),
        "wk": attn_w_qkv[0, :, D_MODEL:2 * D_MODEL].astype(BF16),
        "wv": attn_w_qkv[0, :, 2 * D_MODEL:3 * D_MODEL].astype(BF16),
        "wkt": attn_w_qkv[0, :, D_MODEL:2 * D_MODEL].T.astype(BF16),
        "wvt": attn_w_qkv[0, :, 2 * D_MODEL:3 * D_MODEL].T.astype(BF16),
        "q_gain": jnp.tile(attn_q_gain[0], N_HEADS)[None], "k_gain": jnp.tile(attn_k_gain[0], N_HEADS)[None],
        "k_gain_col": attn_k_gain[0][:, None],
        "bias": attn_logit_bias[0],
        "bias_rows": jnp.repeat(attn_logit_bias[0], dec_seq)[:, None],
        "tri": jnp.tril(jnp.ones((MXU_DIM, MXU_DIM), BF16)),
        "w_o": attn_w_o[0].astype(BF16),
        "w_in": ssm_w_in[0].astype(BF16), "d_skip": ssm_d[0][None], "w_glu": ssm_w_glu[0].astype(BF16),
        "w_up": ffn_w_up.astype(BF16), "conv_w": ffn_conv_w, "conv_b": ffn_conv_b[:, None, :],
        "w_down": ffn_w_down.astype(BF16),
    }
    tables = _s5_tables(ssm_lambda_re[0], ssm_lambda_im[0], ssm_log_dt[0], ssm_b_re[0], ssm_b_im[0],
                        ssm_c_re[0], ssm_c_im[0])

    y_p, k_p, v_p, sr_p, si_p, conv_p = _trunk(x_prompt, w, tables, tm=512, tq=256, tt=64)

    n_pool = cache_k.shape[1]
    to_pages = lambda c: jnp.transpose(c[0], (0, 2, 3, 1)).reshape(n_pool, D_MODEL, PAGE_SIZE)
    past = (to_pages(cache_k), to_pages(cache_v), page_table, 16)
    conv_prev = [_short_seq_prev_rows(state_ffn_conv[i], dec_seq) for i in range(state_ffn_conv.shape[0])]
    y_s, k_s, v_s, sr_s, si_s, conv_s = _trunk(
        x_sample, w, tables, tm=dec_batch * dec_seq, tq=None, tt=dec_seq,
        past=past, ssm_prev=(state_ssm_re[0], state_ssm_im[0]), conv_prev=conv_prev)

    return (y_p, y_s, k_p, v_p, k_s, v_s, sr_p, si_p, sr_s, si_s, conv_p, conv_s)
```

```python
import functools
import math

import jax
import jax.numpy as jnp
from jax import lax
from jax.experimental import pallas as pl
from jax.experimental.pallas import tpu as pltpu

F32 = jnp.float32
BF16 = jnp.bfloat16

D_MODEL = 1024
N_HEADS = 16
HEAD_DIM = 64
PAGE_SIZE = 128
SSM_GROUP = 16
N_GROUPS = D_MODEL // SSM_GROUP
STATE_DIM = 64
D_FF = 2816
NORM_EPS = 1e-6

LANES = 128
SUBLANES = 8
MXU_DIM = 256
VMEM_LIMIT = 56 * 1024 * 1024

HEADS_PER_SLAB = LANES // HEAD_DIM
SSM_BLOCK_IN = MXU_DIM
SSM_GROUPS_PER_BLOCK = SSM_BLOCK_IN // SSM_GROUP
SSM_BLOCK_STATE = SSM_GROUPS_PER_BLOCK * STATE_DIM
N_SSM_BLOCKS = D_MODEL // SSM_BLOCK_IN
SSM_SCAN_LANES = 512
FF_CHUNK = MXU_DIM
LOG2_E = math.log2(math.e)
MASKED_LOGIT = -1e30


def _dot(a, b):
    return jnp.dot(a, b, preferred_element_type=F32)


def _dot_nt(a, b):
    return lax.dot_general(a, b, (((1,), (1,)), ((), ())), preferred_element_type=F32)


def _rms(x, gain):
    return x * lax.rsqrt(jnp.mean(x * x, axis=-1, keepdims=True) + NORM_EPS) * gain


def _resident(shape):
    nd = len(shape)
    return pl.BlockSpec(shape, lambda *_: (0,) * nd, pipeline_mode=pl.Buffered(1))


def _neg_softplus(z):
    return -(jnp.maximum(z, 0.0) + jnp.log(1.0 + jnp.exp(-jnp.abs(z))))


def _head_norm_rows(t, gain):
    r = lax.broadcasted_iota(jnp.int32, (MXU_DIM, MXU_DIM), 0) // HEAD_DIM
    c = lax.broadcasted_iota(jnp.int32, (MXU_DIM, MXU_DIM), 1) // HEAD_DIM
    seg = (r == c).astype(BF16)
    sq = (t * t).astype(BF16)
    ms = jnp.concatenate(
        [_dot(sq[:, j * MXU_DIM:(j + 1) * MXU_DIM], seg) for j in range(D_MODEL // MXU_DIM)],
        axis=1) * (1.0 / HEAD_DIM)
    return t * lax.rsqrt(ms + NORM_EPS) * gain


def _qkv_body(x_ref, gm_ref, wq_ref, wk_ref, wv_ref, qg_ref, kg_ref, q_ref, k_ref, v_ref):
    h = _rms(x_ref[...], gm_ref[...]).astype(BF16)
    q = _head_norm_rows(_dot(h, wq_ref[...]), qg_ref[...])
    q_ref[...] = (q * (HEAD_DIM ** -0.5)).astype(BF16)
    k_ref[...] = _head_norm_rows(_dot(h, wk_ref[...]), kg_ref[...])
    v_ref[...] = _dot(h, wv_ref[...])


def _qkv(x2d, g_mix, wq, wk, wv, q_gain, k_gain, tm):
    t = x2d.shape[0]
    tok = pl.BlockSpec((tm, D_MODEL), lambda i: (i, 0))
    wspec = _resident((D_MODEL, D_MODEL))
    return pl.pallas_call(
        _qkv_body,
        grid=(t // tm,),
        in_specs=[tok, _resident((1, D_MODEL)), wspec, wspec, wspec,
                  _resident((1, D_MODEL)), _resident((1, D_MODEL))],
        out_specs=[tok, tok, tok],
        out_shape=[jax.ShapeDtypeStruct((t, D_MODEL), BF16),
                   jax.ShapeDtypeStruct((t, D_MODEL), F32),
                   jax.ShapeDtypeStruct((t, D_MODEL), F32)],
        compiler_params=pltpu.CompilerParams(
            dimension_semantics=("arbitrary",), vmem_limit_bytes=VMEM_LIMIT),
        name="qkv_proj",
    )(x2d, g_mix, wq, wk, wv, q_gain, k_gain)


def _qkv_t_body(x_ref, gm_ref, wq_ref, wkt_ref, wvt_ref, qg_ref, kgt_ref,
                q_ref, kt_ref, vt_ref, ktb_ref, vtb_ref):
    tm = x_ref.shape[0]
    h = _rms(x_ref[...], gm_ref[...]).astype(BF16)
    q = _head_norm_rows(_dot(h, wq_ref[...]), qg_ref[...])
    q_ref[...] = (q * (HEAD_DIM ** -0.5 * LOG2_E)).astype(BF16)
    k3 = _dot_nt(wkt_ref[...], h).reshape(N_HEADS, HEAD_DIM, tm)
    ms = jnp.mean(k3 * k3, axis=1, keepdims=True)
    kt = (k3 * lax.rsqrt(ms + NORM_EPS) * kgt_ref[...][None]).reshape(D_MODEL, tm)
    kt_ref[...] = kt
    ktb_ref[...] = kt.astype(BF16)
    vt = _dot_nt(wvt_ref[...], h)
    vt_ref[...] = vt
    vtb_ref[...] = vt.astype(BF16)


def _qkv_t(x2d, g_mix, wq, wkt, wvt, q_gain, k_gain_col, seq_len, tm):
    t = x2d.shape[0]
    tiles_per_seq = seq_len // tm
    tok = pl.BlockSpec((tm, D_MODEL), lambda i: (i, 0))
    feat = pl.BlockSpec((None, D_MODEL, tm), lambda i: (i // tiles_per_seq, 0, i % tiles_per_seq))
    wspec = _resident((D_MODEL, D_MODEL))
    shape_t = (t // seq_len, D_MODEL, seq_len)
    return pl.pallas_call(
        _qkv_t_body,
        grid=(t // tm,),
        in_specs=[tok, _resident((1, D_MODEL)), wspec, wspec, wspec,
                  _resident((1, D_MODEL)), _resident((HEAD_DIM, 1))],
        out_specs=[tok, feat, feat, feat, feat],
        out_shape=[jax.ShapeDtypeStruct((t, D_MODEL), BF16),
                   jax.ShapeDtypeStruct(shape_t, F32),
                   jax.ShapeDtypeStruct(shape_t, F32),
                   jax.ShapeDtypeStruct(shape_t, BF16),
                   jax.ShapeDtypeStruct(shape_t, BF16)],
        compiler_params=pltpu.CompilerParams(
            dimension_semantics=("arbitrary",), vmem_limit_bytes=VMEM_LIMIT),
        name="qkv_proj_t",
    )(x2d, g_mix, wq, wkt, wvt, q_gain, k_gain_col)


def _attn_prompt_body(bias_ref, tri_ref, q_ref, kt_ref, vt_ref, o_ref, acc_ref, z_scr, i_scr,
                      *, tq, n_slabs):
    grp = pl.program_id(1)
    i = pl.program_id(2)
    n_heads = n_slabs * HEADS_PER_SLAB
    low = lax.broadcasted_iota(jnp.int32, (tq, LANES), 1) < HEAD_DIM
    q_heads = []
    for s in range(n_slabs):
        q = q_ref[:, s * LANES:(s + 1) * LANES]
        q_heads += [jnp.where(low, q, jnp.zeros_like(q)), jnp.where(low, jnp.zeros_like(q), q)]
    bias = [bias_ref[grp * n_heads + h] for h in range(n_heads)]
    row = lax.broadcasted_iota(jnp.int32, (tq, tq), 0)
    col = lax.broadcasted_iota(jnp.int32, (tq, tq), 1)
    causal = col < row

    acc_ref[...] = jnp.zeros_like(acc_ref)

    def rows_of(h):
        return slice((h // HEADS_PER_SLAB) * LANES, (h // HEADS_PER_SLAB + 1) * LANES)

    def stage_a(j, diag):
        keys = pl.ds(pl.multiple_of(j * tq, tq), tq)
        totals = []
        for h in range(n_heads):
            z = _dot(q_heads[h], kt_ref[rows_of(h), keys]) + bias[h]
            if diag:
                z = jnp.where(causal, z, MASKED_LOGIT)
            sp = jnp.maximum(z, 0.0) + jnp.log2(1.0 + jnp.exp2(-jnp.abs(z)))
            incl = _dot(sp.astype(BF16), tri_ref[...])
            z_scr[h] = z
            i_scr[h] = incl
            totals.append(incl[:, 0:1])
        return tuple(totals)

    def stage_b(j, carry):
        keys = pl.ds(pl.multiple_of(j * tq, tq), tq)
        for h in range(n_heads):
            w = jnp.exp2(z_scr[h] + i_scr[h] + carry[h])
            acc_ref[h] += _dot_nt(w.astype(BF16), vt_ref[rows_of(h), keys])

    def step(jj, state):
        carry, totals = state
        j = i - 1 - jj
        stage_b(j + 1, carry)
        carry = tuple(c + t for c, t in zip(carry, totals))
        return carry, stage_a(j, False)

    zero = jnp.zeros((tq, 1), F32)
    carry, _ = lax.fori_loop(0, i, step, ((zero,) * n_heads, stage_a(i, True)))
    stage_b(0, carry)
    for s in range(n_slabs):
        o_ref[:, s * LANES:(s + 1) * LANES] = jnp.where(
            low, acc_ref[HEADS_PER_SLAB * s], acc_ref[HEADS_PER_SLAB * s + 1]).astype(BF16)


def _attn_prompt(q, kt, vt, bias, tri, tq, n_slabs):
    b, s, _ = q.shape
    width = n_slabs * LANES
    kv_spec = pl.BlockSpec((None, width, s), lambda bi, gi, qi: (bi, gi, 0))
    q_spec = pl.BlockSpec((None, tq, width), lambda bi, gi, qi: (bi, qi, gi))
    return pl.pallas_call(
        functools.partial(_attn_prompt_body, tq=tq, n_slabs=n_slabs),
        grid=(b, D_MODEL // width, s // tq),
        in_specs=[pl.BlockSpec(memory_space=pltpu.SMEM),
                  pl.BlockSpec((tq, tq), lambda bi, gi, qi: (0, 0)), q_spec, kv_spec, kv_spec],
        out_specs=q_spec,
        out_shape=jax.ShapeDtypeStruct((b, s, D_MODEL), BF16),
        scratch_shapes=[pltpu.VMEM((n_slabs * HEADS_PER_SLAB, tq, LANES), F32),
                        pltpu.VMEM((n_slabs * HEADS_PER_SLAB, tq, tq), F32),
                        pltpu.VMEM((n_slabs * HEADS_PER_SLAB, tq, tq), F32)],
        compiler_params=pltpu.CompilerParams(
            dimension_semantics=("arbitrary", "arbitrary", "arbitrary"),
            vmem_limit_bytes=VMEM_LIMIT),
        name="attn_prompt",
    )(bias, tri, q, kt, vt)


def _attn_sample_body(pt_ref, q_ref, kn_ref, vn_ref, bias_ref, tri_ref, *rest, n_q, pps):
    del pt_ref
    kt_refs, vt_refs = rest[:pps], rest[pps:2 * pps]
    o_ref, qbd_ref, kt_scr, vt_scr, w_scr, acc_ref, carry_ref = rest[2 * pps:]
    step = pl.program_id(1)
    n_rows = N_HEADS * n_q
    window = pps * PAGE_SIZE

    @pl.when(step == 0)
    def _():
        q_rep = jnp.concatenate([q_ref[...].astype(F32)] * N_HEADS, axis=0)
        rh = lax.broadcasted_iota(jnp.int32, (n_rows, D_MODEL), 0) // n_q
        ch = lax.broadcasted_iota(jnp.int32, (n_rows, D_MODEL), 1) // HEAD_DIM
        qbd = jnp.where(rh == ch, q_rep, 0.0).astype(BF16)
        qbd_ref[...] = qbd
        pad = jnp.zeros((PAGE_SIZE - n_q, D_MODEL), F32)
        kn = jnp.concatenate([kn_ref[...], pad], axis=0).astype(BF16)
        vn = jnp.concatenate([vn_ref[...], pad], axis=0).astype(BF16)
        z = _dot_nt(qbd, kn) + bias_ref[...]
        key = lax.broadcasted_iota(jnp.int32, (n_rows, PAGE_SIZE), 1)
        qi = lax.broadcasted_iota(jnp.int32, (n_rows, PAGE_SIZE), 0) % n_q
        visible = key < qi
        lk = jnp.where(visible, _neg_softplus(z), 0.0)
        incl = _dot(lk.astype(BF16), tri_ref[0:PAGE_SIZE, 0:PAGE_SIZE])
        w = jnp.where(visible, jnp.exp(z + incl), 0.0)
        acc_ref[...] = _dot(w.astype(BF16), vn)
        carry_ref[...] = jnp.broadcast_to(incl[:, 0:1], carry_ref.shape)

    for i in range(pps):
        lanes = slice(i * PAGE_SIZE, (i + 1) * PAGE_SIZE)
        kt_scr[:, lanes] = kt_refs[i][...].astype(BF16)
        vt_scr[:, lanes] = vt_refs[i][...].astype(BF16)
    z_all = _dot(qbd_ref[...], kt_scr[...]) + bias_ref[...]
    carry = carry_ref[:, 0:1]
    for ch in reversed(range(window // MXU_DIM)):
        lanes = slice(ch * MXU_DIM, (ch + 1) * MXU_DIM)
        z = z_all[:, lanes]
        incl = _dot(_neg_softplus(z).astype(BF16), tri_ref[...])
        w_scr[:, lanes] = jnp.exp(z + incl + carry).astype(BF16)
        carry = carry + incl[:, 0:1]
    acc_ref[...] += _dot_nt(w_scr[...], vt_scr[...])
    carry_ref[...] = jnp.broadcast_to(carry, carry_ref.shape)

    @pl.when(step == pl.num_programs(1) - 1)
    def _():
        rh = lax.broadcasted_iota(jnp.int32, (n_rows, D_MODEL), 0) // n_q
        ch = lax.broadcasted_iota(jnp.int32, (n_rows, D_MODEL), 1) // HEAD_DIM
        own = jnp.where(rh == ch, acc_ref[...], 0.0)
        out = own[0:n_q, :]
        for h in range(1, N_HEADS):
            out = out + own[h * n_q:(h + 1) * n_q, :]
        o_ref[...] = out.astype(BF16)


def _attn_sample(q, k_new, v_new, bias_rows, tri, cache_kt, cache_vt, page_table, pps):
    db, n_q, _ = q.shape
    n_pages = page_table.shape[1]
    n_rows = N_HEADS * n_q
    window = pps * PAGE_SIZE

    def page_spec(i):
        def idx(bi, si, pt):
            return (pt[bi, n_pages - pps * (si + 1) + i], 0, 0)
        return pl.BlockSpec((None, D_MODEL, PAGE_SIZE), idx)

    per_seq = lambda bi, si, pt: (bi, 0, 0)
    const = lambda bi, si, pt: (0, 0)
    pages = [page_spec(i) for i in range(pps)]
    grid_spec = pltpu.PrefetchScalarGridSpec(
        num_scalar_prefetch=1,
        grid=(db, n_pages // pps),
        in_specs=[pl.BlockSpec((None, n_q, D_MODEL), per_seq),
                  pl.BlockSpec((None, n_q, D_MODEL), per_seq),
                  pl.BlockSpec((None, n_q, D_MODEL), per_seq),
                  pl.BlockSpec((n_rows, 1), const),
                  pl.BlockSpec((MXU_DIM, MXU_DIM), const)] + pages + pages,
        out_specs=pl.BlockSpec((None, n_q, D_MODEL), per_seq),
        scratch_shapes=[pltpu.VMEM((n_rows, D_MODEL), BF16),
                        pltpu.VMEM((D_MODEL, window), BF16),
                        pltpu.VMEM((D_MODEL, window), BF16),
                        pltpu.VMEM((n_rows, window), BF16),
                        pltpu.VMEM((n_rows, D_MODEL), F32),
                        pltpu.VMEM((n_rows, LANES), F32)])
    return pl.pallas_call(
        functools.partial(_attn_sample_body, n_q=n_q, pps=pps),
        grid_spec=grid_spec,
        out_shape=jax.ShapeDtypeStruct((db, n_q, D_MODEL), BF16),
        compiler_params=pltpu.CompilerParams(
            dimension_semantics=("arbitrary", "arbitrary"), vmem_limit_bytes=VMEM_LIMIT),
        name="attn_sample",
    )(page_table, q, k_new, v_new, bias_rows, tri, *([cache_kt] * pps), *([cache_vt] * pps))


def _ffn_body(*refs, with_wo, short_seq, tiles_per_seq, seq_len):
    refs = list(refs)
    x_ref = refs.pop(0)
    if with_wo:
        o_ref, wo_ref = refs.pop(0), refs.pop(0)
    g_ref, wup_ref, cw_ref, cb_ref, wdn_ref = [refs.pop(0) for _ in range(5)]
    if short_seq:
        p1_ref, p2_ref = refs.pop(0), refs.pop(0)
    out_ref, conv_ref, a_scr, g_scr = refs
    tm = x_ref.shape[0]

    x1 = x_ref[...]
    if with_wo:
        x1 = x1 + _dot(o_ref[...], wo_ref[...])
    h = _rms(x1, g_ref[...]).astype(BF16)

    if short_seq:
        a_scr[0:SUBLANES, :] = jnp.zeros((SUBLANES, D_FF), F32)
    else:
        @pl.when(pl.program_id(0) % tiles_per_seq == 0)
        def _():
            a_scr[0:SUBLANES, :] = jnp.zeros((SUBLANES, D_FF), F32)

    a_scr[SUBLANES:SUBLANES + tm, :] = _dot(h, wup_ref[:, 0:D_FF])

    row = lax.broadcasted_iota(jnp.int32, (tm, FF_CHUNK), 0)
    for c in range(D_FF // FF_CHUNK):
        cs = slice(c * FF_CHUNK, (c + 1) * FF_CHUNK)
        a0 = a_scr[SUBLANES:SUBLANES + tm, cs]
        a1 = a_scr[SUBLANES - 1:SUBLANES - 1 + tm, cs]
        a2 = a_scr[SUBLANES - 2:SUBLANES - 2 + tm, cs]
        if short_seq:
            pos = row % seq_len
            a1 = jnp.where(pos < 1, p1_ref[:, cs], a1)
            a2 = jnp.where(pos < 2, p2_ref[:, cs], a2)
        b = _dot(h, wup_ref[:, D_FF + c * FF_CHUNK:D_FF + (c + 1) * FF_CHUNK])
        cc = cb_ref[:, cs] + cw_ref[0:1, cs] * a2 + cw_ref[1:2, cs] * a1 + cw_ref[2:3, cs] * a0
        g_scr[:, cs] = (cc * jax.nn.sigmoid(cc) * b).astype(BF16)

    tail = a_scr[tm:tm + SUBLANES, :]
    if short_seq:
        conv_ref[...] = a_scr[SUBLANES:SUBLANES + tm, :]
    else:
        conv_ref[...] = tail
        a_scr[0:SUBLANES, :] = tail
    out_ref[...] = x1 + _dot(g_scr[...], wdn_ref[...])


def _layer_resident(shape, layer):
    nd = len(shape)
    return pl.BlockSpec((None,) + tuple(shape), lambda *_: (layer,) + (0,) * nd,
                        pipeline_mode=pl.Buffered(1))


def _ffn(x2d, o2d, w_o, g_ffn, w_up, conv_w, conv_b, w_down, layer, seq_len, tm, prev_rows=None):
    t = x2d.shape[0]
    with_wo = o2d is not None
    short_seq = prev_rows is not None
    n_tiles = t // tm
    row = lambda i: (i, 0)
    tok = pl.BlockSpec((tm, D_MODEL), row)
    args, specs = [x2d], [tok]
    if with_wo:
        args += [o2d, w_o]
        specs += [tok, _resident((D_MODEL, D_MODEL))]
    args += [g_ffn, w_up, conv_w, conv_b, w_down]
    specs += [_layer_resident((1, D_MODEL), layer), _layer_resident((D_MODEL, 2 * D_FF), layer),
              _layer_resident((3, D_FF), layer), _layer_resident((1, D_FF), layer),
              _layer_resident((D_FF, D_MODEL), layer)]
    if short_seq:
        args += list(prev_rows)
        specs += [pl.BlockSpec((tm, D_FF), row)] * 2
        conv_shape, conv_spec = (t, D_FF), pl.BlockSpec((tm, D_FF), row)
        tiles_per_seq = 1
    else:
        conv_shape = (n_tiles, SUBLANES, D_FF)
        conv_spec = pl.BlockSpec((None, SUBLANES, D_FF), lambda i: (i, 0, 0))
        tiles_per_seq = seq_len // tm
    return pl.pallas_call(
        functools.partial(_ffn_body, with_wo=with_wo, short_seq=short_seq,
                          tiles_per_seq=tiles_per_seq, seq_len=seq_len),
        grid=(n_tiles,),
        in_specs=specs,
        out_specs=[tok, conv_spec],
        out_shape=[jax.ShapeDtypeStruct((t, D_MODEL), F32),
                   jax.ShapeDtypeStruct(conv_shape, F32)],
        scratch_shapes=[pltpu.VMEM((tm + SUBLANES, D_FF), F32), pltpu.VMEM((tm, D_FF), BF16)],
        compiler_params=pltpu.CompilerParams(
            dimension_semantics=("arbitrary",), vmem_limit_bytes=VMEM_LIMIT),
        name="conv_ffn",
    )(*args)


def _s5_body(x_ref, g_ref, win_ref, bd_ref, cd_ref, are_ref, aim_ref, dskip_ref, wglu_ref, s0_ref,
             xo_ref, sfin_ref, xt_scr, st_scr, bu_scr, y_scr, *, tt):
    ti = pl.program_id(1)
    n_seq = SUBLANES

    @pl.when(ti == 0)
    def _():
        st_scr[...] = s0_ref[...]

    n_slabs = D_MODEL // LANES
    for b in range(n_seq):
        for j in range(n_slabs):
            xt_scr[j, pl.ds(b, tt, stride=n_seq), :] = x_ref[b, :, j * LANES:(j + 1) * LANES]
    xt = jnp.concatenate([xt_scr[j] for j in range(n_slabs)], axis=1)
    u = _dot(_rms(xt, g_ref[...]).astype(BF16), win_ref[...])
    ub = u.astype(BF16)

    for k in range(N_SSM_BLOCKS):
        bu_scr[...] = _dot(ub[:, k * SSM_BLOCK_IN:(k + 1) * SSM_BLOCK_IN], bd_ref[k])
        for lc in range(SSM_BLOCK_STATE // SSM_SCAN_LANES):
            re_l = slice(lc * SSM_SCAN_LANES, (lc + 1) * SSM_SCAN_LANES)
            im_l = slice(SSM_BLOCK_STATE + lc * SSM_SCAN_LANES,
                         SSM_BLOCK_STATE + (lc + 1) * SSM_SCAN_LANES)
            base = k * 2 * SSM_BLOCK_STATE
            st_re = slice(base + re_l.start, base + re_l.stop)
            st_im = slice(base + im_l.start, base + im_l.stop)
            ar = jnp.broadcast_to(are_ref[k, :, re_l], (n_seq, SSM_SCAN_LANES))
            ai = jnp.broadcast_to(aim_ref[k, :, re_l], (n_seq, SSM_SCAN_LANES))

            def step(t, s, ar=ar, ai=ai, re_l=re_l, im_l=im_l):
                sr, si = s
                r0 = pl.multiple_of(t * n_seq, n_seq)
                nr = ar * sr - ai * si + bu_scr[pl.ds(r0, n_seq), re_l]
                ni = ar * si + ai * sr + bu_scr[pl.ds(r0, n_seq), im_l]
                bu_scr[pl.ds(r0, n_seq), re_l] = nr
                bu_scr[pl.ds(r0, n_seq), im_l] = ni
                return nr, ni

            sr, si = lax.fori_loop(0, tt, step, (st_scr[:, st_re], st_scr[:, st_im]), unroll=True)
            st_scr[:, st_re] = sr
            st_scr[:, st_im] = si
        y_scr[:, k * SSM_BLOCK_IN:(k + 1) * SSM_BLOCK_IN] = _dot(bu_scr[...].astype(BF16), cd_ref[k])

    y = y_scr[...] + dskip_ref[...] * u
    gg = _dot(jax.nn.gelu(y).astype(BF16), wglu_ref[...])
    x_new = xt + gg[:, 0:D_MODEL] * jax.nn.sigmoid(gg[:, D_MODEL:2 * D_MODEL])
    for j in range(n_slabs):
        xt_scr[j] = x_new[:, j * LANES:(j + 1) * LANES]
    for b in range(n_seq):
        for j in range(n_slabs):
            xo_ref[b, :, j * LANES:(j + 1) * LANES] = xt_scr[j, pl.ds(b, tt, stride=n_seq), :]

    @pl.when(ti == pl.num_programs(1) - 1)
    def _():
        sfin_ref[...] = st_scr[...]


def _s5(x, g_mix, w_in, bd, cd, a_re, a_im, d_skip, w_glu, s0, tt):
    b, s, _ = x.shape
    n_state = N_SSM_BLOCKS * 2 * SSM_BLOCK_STATE
    x_spec = pl.BlockSpec((SUBLANES, tt, D_MODEL), lambda bi, ti: (bi, ti, 0))
    s_spec = pl.BlockSpec((None, SUBLANES, n_state), lambda bi, ti: (bi, 0, 0))
    rows = tt * SUBLANES
    return pl.pallas_call(
        functools.partial(_s5_body, tt=tt),
        grid=(b // SUBLANES, s // tt),
        in_specs=[x_spec, _resident((1, D_MODEL)), _resident((D_MODEL, D_MODEL)),
                  _resident(bd.shape), _resident(cd.shape), _resident(a_re.shape),
                  _resident(a_im.shape), _resident((1, D_MODEL)),
                  _resident((D_MODEL, 2 * D_MODEL)), s_spec],
        out_specs=[x_spec, s_spec],
        out_shape=[jax.ShapeDtypeStruct((b, s, D_MODEL), F32),
                   jax.ShapeDtypeStruct((b // SUBLANES, SUBLANES, n_state), F32)],
        scratch_shapes=[pltpu.VMEM((D_MODEL // LANES, rows, LANES), F32),
                        pltpu.VMEM((SUBLANES, n_state), F32),
                        pltpu.VMEM((rows, 2 * SSM_BLOCK_STATE), F32),
                        pltpu.VMEM((rows, D_MODEL), F32)],
        compiler_params=pltpu.CompilerParams(
            dimension_semantics=("arbitrary", "arbitrary"), vmem_limit_bytes=VMEM_LIMIT),
        name="s5_layer",
    )(x, g_mix, w_in, bd, cd, a_re, a_im, d_skip, w_glu, s0)


def _s5_tables(lam_re, lam_im, log_dt, b_re, b_im, c_re, c_im):
    dt = jnp.exp(log_dt)[:, None]
    mag = jnp.exp(lam_re * dt)
    a_re = mag * jnp.cos(lam_im * dt)
    a_im = mag * jnp.sin(lam_im * dt)
    den = lam_re * lam_re + lam_im * lam_im
    zr = ((a_re - 1.0) * lam_re + a_im * lam_im) / den
    zi = (a_im * lam_re - (a_re - 1.0) * lam_im) / den
    bbar_re = zr[..., None] * b_re - zi[..., None] * b_im
    bbar_im = zr[..., None] * b_im + zi[..., None] * b_re
    eye = jnp.eye(SSM_GROUPS_PER_BLOCK, dtype=F32)

    def pack_in(m):
        m = m.reshape(N_SSM_BLOCKS, SSM_GROUPS_PER_BLOCK, STATE_DIM, SSM_GROUP)
        full = jnp.einsum("kgpc,gh->kgchp", m, eye)
        return full.reshape(N_SSM_BLOCKS, SSM_BLOCK_IN, SSM_BLOCK_STATE)

    def pack_out(m):
        m = m.reshape(N_SSM_BLOCKS, SSM_GROUPS_PER_BLOCK, SSM_GROUP, STATE_DIM)
        full = jnp.einsum("kgcp,gh->kgphc", m, eye)
        return full.reshape(N_SSM_BLOCKS, SSM_BLOCK_STATE, SSM_BLOCK_IN)

    bd = jnp.concatenate([pack_in(bbar_re), pack_in(bbar_im)], axis=2).astype(BF16)
    cd = jnp.concatenate([pack_out(c_re), pack_out(-c_im)], axis=1).astype(BF16)
    shape = (N_SSM_BLOCKS, 1, SSM_BLOCK_STATE)
    return bd, cd, a_re.reshape(shape), a_im.reshape(shape)


def _pack_state(s_re, s_im):
    n = s_re.shape[0]
    packed = jnp.concatenate([s_re.reshape(n, N_SSM_BLOCKS, SSM_BLOCK_STATE),
                              s_im.reshape(n, N_SSM_BLOCKS, SSM_BLOCK_STATE)], axis=2)
    return packed.reshape(n // SUBLANES, SUBLANES, -1)


def _unpack_state(packed, n):
    s = packed.reshape(n, N_SSM_BLOCKS, 2, SSM_BLOCK_STATE)
    return (s[:, :, 0].reshape(n, N_GROUPS, STATE_DIM), s[:, :, 1].reshape(n, N_GROUPS, STATE_DIM))


def _trunk(x, w, tables, *, tm, tq, tt, past=None, ssm_prev=None, conv_prev=None):
    b, s, _ = x.shape
    t = b * s
    x2d = x.reshape(t, D_MODEL)

    shape3 = (b, s, D_MODEL)
    heads = (b, s, N_HEADS, HEAD_DIM)
    if past is None:
        q, kt, vt, ktb, vtb = _qkv_t(x2d, w["norm_mix"][0], w["wq"], w["wkt"], w["wvt"], w["q_gain"],
                                     w["k_gain_col"], s, tm)
        o = _attn_prompt(q.reshape(shape3), ktb, vtb, w["bias"] * LOG2_E, -w["tri"], tq, 8)
        feat_major = lambda a: jnp.transpose(a.reshape(b, N_HEADS, HEAD_DIM, s), (0, 3, 1, 2))
        k, v = feat_major(kt), feat_major(vt)
    else:
        q, k, v = _qkv(x2d, w["norm_mix"][0], w["wq"], w["wk"], w["wv"], w["q_gain"], w["k_gain"], tm)
        cache_kt, cache_vt, page_table, pps = past
        o = _attn_sample(q.reshape(shape3), k.reshape(shape3), v.reshape(shape3), w["bias_rows"],
                         w["tri"], cache_kt, cache_vt, page_table, pps)
        k, v = k.reshape(heads), v.reshape(heads)
    prev0 = None if conv_prev is None else conv_prev[0]
    x2d, conv0 = _ffn(x2d, o.reshape(t, D_MODEL), w["w_o"], w["norm_ffn"], w["w_up"],
                      w["conv_w"], w["conv_b"], w["w_down"], 0, s, tm, prev0)

    bd, cd, a_re, a_im = tables
    if ssm_prev is None:
        s0 = jnp.zeros((b // SUBLANES, SUBLANES, N_SSM_BLOCKS * 2 * SSM_BLOCK_STATE), F32)
    else:
        s0 = _pack_state(*ssm_prev)
    x3, s_fin = _s5(x2d.reshape(shape3), w["norm_mix"][1], w["w_in"], bd, cd, a_re, a_im,
                    w["d_skip"], w["w_glu"], s0, tt)
    s_re, s_im = _unpack_state(s_fin, b)
    prev1 = None if conv_prev is None else conv_prev[1]
    y2d, conv1 = _ffn(x3.reshape(t, D_MODEL), None, None, w["norm_ffn"], w["w_up"],
                      w["conv_w"], w["conv_b"], w["w_down"], 1, s, tm, prev1)

    def conv_state(c):
        if conv_prev is None:
            per_seq = c.reshape(b, s // tm, SUBLANES, D_FF)[:, -1]
        else:
            per_seq = c.reshape(b, s, D_FF)
        return per_seq[:, -2:]

    return (y2d.reshape(shape3), k[None], v[None], s_re[None], s_im[None],
            jnp.stack([conv_state(conv0), conv_state(conv1)]))


def _short_seq_prev_rows(prev, seq_len):
    bsz = prev.shape[0]
    zeros = jnp.zeros((bsz, seq_len - 2, D_FF), F32)
    p2 = jnp.concatenate([prev, zeros], axis=1)
    p1 = jnp.concatenate([prev[:, 1:2], jnp.zeros((bsz, seq_len - 1, D_FF), F32)], axis=1)
    return p1.reshape(bsz * seq_len, D_FF), p2.reshape(bsz * seq_len, D_FF)


def kernel(x_prompt, x_sample, cache_k, cache_v, state_ssm_re, state_ssm_im, state_ffn_conv, page_table,
           norm_mix, norm_ffn, attn_w_qkv, attn_q_gain, attn_k_gain, attn_logit_bias, attn_w_o,
           ssm_w_in, ssm_lambda_re, ssm_lambda_im, ssm_log_dt, ssm_b_re, ssm_b_im, ssm_c_re, ssm_c_im,
           ssm_d, ssm_w_glu, ffn_w_up, ffn_conv_w, ffn_conv_b, ffn_w_down):
    dec_batch, dec_seq, _ = x_sample.shape
    w = {
        "norm_mix": norm_mix[:, None, :], "norm_ffn": norm_ffn[:, None, :],
        "wq": attn_w_qkv[0, :, 0:D_MODEL].astype(BF16),
        "wk": attn_w_qkv[0, :, D_MODEL:2 * D_MODEL].astype(BF16),
        "wv": attn_w_qkv[0, :, 2 * D_MODEL:3 * D_MODEL].astype(BF16),
        "wkt": attn_w_qkv[0, :, D_MODEL:2 * D_MODEL].T.astype(BF16),
        "wvt": attn_w_qkv[0, :, 2 * D_MODEL:3 * D_MODEL].T.astype(BF16),
        "q_gain": jnp.tile(attn_q_gain[0], N_HEADS)[None], "k_gain": jnp.tile(attn_k_gain[0], N_HEADS)[None],
        "k_gain_col": attn_k_gain[0][:, None],
        "bias": attn_logit_bias[0],
        "bias_rows": jnp.repeat(attn_logit_bias[0], dec_seq)[:, None],
        "tri": jnp.tril(jnp.ones((MXU_DIM, MXU_DIM), BF16)),
        "w_o": attn_w_o[0].astype(BF16),
        "w_in": ssm_w_in[0].astype(BF16), "d_skip": ssm_d[0][None], "w_glu": ssm_w_glu[0].astype(BF16),
        "w_up": ffn_w_up.astype(BF16), "conv_w": ffn_conv_w, "conv_b": ffn_conv_b[:, None, :],
        "w_down": ffn_w_down.astype(BF16),
    }
    tables = _s5_tables(ssm_lambda_re[0], ssm_lambda_im[0], ssm_log_dt[0], ssm_b_re[0], ssm_b_im[0],
                        ssm_c_re[0], ssm_c_im[0])

    y_p, k_p, v_p, sr_p, si_p, conv_p = _trunk(x_prompt, w, tables, tm=512, tq=256, tt=64)

    n_pool = cache_k.shape[1]
    to_pages = lambda c: jnp.transpose(c[0], (0, 2, 3, 1)).reshape(n_pool, D_MODEL, PAGE_SIZE)
    past = (to_pages(cache_k), to_pages(cache_v), page_table, 16)
    conv_prev = [_short_seq_prev_rows(state_ffn_conv[i], dec_seq) for i in range(state_ffn_conv.shape[0])]
    y_s, k_s, v_s, sr_s, si_s, conv_s = _trunk(
        x_sample, w, tables, tm=dec_batch * dec_seq, tq=None, tt=dec_seq,
        past=past, ssm_prev=(state_ssm_re[0], state_ssm_im[0]), conv_prev=conv_prev)

    return (y_p, y_s, k_p, v_p, k_s, v_s, sr_p, si_p, sr_s, si_s, conv_p, conv_s)
```

```python
import functools
import math

import jax
import jax.numpy as jnp
from jax import lax
from jax.experimental import pallas as pl
from jax.experimental.pallas import tpu as pltpu

F32 = jnp.float32
BF16 = jnp.bfloat16

D_MODEL = 1024
N_HEADS = 16
HEAD_DIM = 64
PAGE_SIZE = 128
SSM_GROUP = 16
N_GROUPS = D_MODEL // SSM_GROUP
STATE_DIM = 64
D_FF = 2816
NORM_EPS = 1e-6

LANES = 128
SUBLANES = 8
MXU_DIM = 256
VMEM_LIMIT = 56 * 1024 * 1024

HEADS_PER_SLAB = LANES // HEAD_DIM
SSM_BLOCK_IN = MXU_DIM
SSM_GROUPS_PER_BLOCK = SSM_BLOCK_IN // SSM_GROUP
SSM_BLOCK_STATE = SSM_GROUPS_PER_BLOCK * STATE_DIM
N_SSM_BLOCKS = D_MODEL // SSM_BLOCK_IN
SSM_SCAN_LANES = 512
FF_CHUNK = MXU_DIM
LOG2_E = math.log2(math.e)
MASKED_LOGIT = -1e30


def _dot(a, b):
    return jnp.dot(a, b, preferred_element_type=F32)


def _dot_nt(a, b):
    return lax.dot_general(a, b, (((1,), (1,)), ((), ())), preferred_element_type=F32)


def _rms(x, gain):
    return x * lax.rsqrt(jnp.mean(x * x, axis=-1, keepdims=True) + NORM_EPS) * gain


def _resident(shape):
    nd = len(shape)
    return pl.BlockSpec(shape, lambda *_: (0,) * nd, pipeline_mode=pl.Buffered(1))


def _neg_softplus(z):
    return -(jnp.maximum(z, 0.0) + jnp.log(1.0 + jnp.exp(-jnp.abs(z))))


def _head_norm_rows(t, gain):
    r = lax.broadcasted_iota(jnp.int32, (MXU_DIM, MXU_DIM), 0) // HEAD_DIM
    c = lax.broadcasted_iota(jnp.int32, (MXU_DIM, MXU_DIM), 1) // HEAD_DIM
    seg = (r == c).astype(BF16)
    sq = (t * t).astype(BF16)
    ms = jnp.concatenate(
        [_dot(sq[:, j * MXU_DIM:(j + 1) * MXU_DIM], seg) for j in range(D_MODEL // MXU_DIM)],
        axis=1) * (1.0 / HEAD_DIM)
    return t * lax.rsqrt(ms + NORM_EPS) * gain


def _qkv_body(x_ref, gm_ref, wq_ref, wk_ref, wv_ref, qg_ref, kg_ref, q_ref, k_ref, v_ref):
    h = _rms(x_ref[...], gm_ref[...]).astype(BF16)
    q = _head_norm_rows(_dot(h, wq_ref[...]), qg_ref[...])
    q_ref[...] = (q * (HEAD_DIM ** -0.5)).astype(BF16)
    k_ref[...] = _head_norm_rows(_dot(h, wk_ref[...]), kg_ref[...])
    v_ref[...] = _dot(h, wv_ref[...])


def _qkv(x2d, g_mix, wq, wk, wv, q_gain, k_gain, tm):
    t = x2d.shape[0]
    tok = pl.BlockSpec((tm, D_MODEL), lambda i: (i, 0))
    wspec = _resident((D_MODEL, D_MODEL))
    return pl.pallas_call(
        _qkv_body,
        grid=(t // tm,),
        in_specs=[tok, _resident((1, D_MODEL)), wspec, wspec, wspec,
                  _resident((1, D_MODEL)), _resident((1, D_MODEL))],
        out_specs=[tok, tok, tok],
        out_shape=[jax.ShapeDtypeStruct((t, D_MODEL), BF16),
                   jax.ShapeDtypeStruct((t, D_MODEL), F32),
                   jax.ShapeDtypeStruct((t, D_MODEL), F32)],
        compiler_params=pltpu.CompilerParams(
            dimension_semantics=("arbitrary",), vmem_limit_bytes=VMEM_LIMIT),
        name="qkv_proj",
    )(x2d, g_mix, wq, wk, wv, q_gain, k_gain)


def _qkv_t_steps(sub, x_ref, gm_ref, wq_ref, wkt_ref, wvt_ref, qg_ref, kgt_ref,
                 q_ref, kt_ref, vt_ref, ktb_ref, vtb_ref, h_scr):
    tm = x_ref.shape[0]

    @pl.when(sub == 0)
    def _():
        h = _rms(x_ref[...], gm_ref[...]).astype(BF16)
        h_scr[...] = h
        q = _head_norm_rows(_dot(h, wq_ref[...]), qg_ref[...])
        q_ref[...] = (q * (HEAD_DIM ** -0.5 * LOG2_E)).astype(BF16)

    @pl.when(sub == 1)
    def _():
        k3 = _dot_nt(wkt_ref[...], h_scr[...]).reshape(N_HEADS, HEAD_DIM, tm)
        ms = jnp.mean(k3 * k3, axis=1, keepdims=True)
        kt = (k3 * lax.rsqrt(ms + NORM_EPS) * kgt_ref[...][None]).reshape(D_MODEL, tm)
        kt_ref[...] = kt
        ktb_ref[...] = kt.astype(BF16)

    @pl.when(sub == 2)
    def _():
        vt = _dot_nt(wvt_ref[...], h_scr[...])
        vt_ref[...] = vt
        vtb_ref[...] = vt.astype(BF16)


def _attn_prompt_body(bias_ref, tri_ref, q_ref, kt_ref, vt_ref, o_ref, acc_ref, z_scr, i_scr,
                      *, tq, n_slabs):
    grp = pl.program_id(1)
    i = pl.program_id(2)
    n_heads = n_slabs * HEADS_PER_SLAB
    low = lax.broadcasted_iota(jnp.int32, (tq, LANES), 1) < HEAD_DIM
    q_heads = []
    for s in range(n_slabs):
        q = q_ref[:, s * LANES:(s + 1) * LANES]
        q_heads += [jnp.where(low, q, jnp.zeros_like(q)), jnp.where(low, jnp.zeros_like(q), q)]
    bias = [bias_ref[grp * n_heads + h] for h in range(n_heads)]
    row = lax.broadcasted_iota(jnp.int32, (tq, tq), 0)
    col = lax.broadcasted_iota(jnp.int32, (tq, tq), 1)
    causal = col < row

    acc_ref[...] = jnp.zeros_like(acc_ref)

    def rows_of(h):
        return slice((h // HEADS_PER_SLAB) * LANES, (h // HEADS_PER_SLAB + 1) * LANES)

    def stage_a(j, diag):
        keys = pl.ds(pl.multiple_of(j * tq, tq), tq)
        totals = []
        for h in range(n_heads):
            z = _dot(q_heads[h], kt_ref[rows_of(h), keys]) + bias[h]
            if diag:
                z = jnp.where(causal, z, MASKED_LOGIT)
            sp = jnp.maximum(z, 0.0) + jnp.log2(1.0 + jnp.exp2(-jnp.abs(z)))
            incl = _dot(sp.astype(BF16), tri_ref[...])
            z_scr[h] = z
            i_scr[h] = incl
            totals.append(incl[:, 0:1])
        return tuple(totals)

    def stage_b(j, carry):
        keys = pl.ds(pl.multiple_of(j * tq, tq), tq)
        for h in range(n_heads):
            w = jnp.exp2(z_scr[h] + i_scr[h] + carry[h])
            acc_ref[h] += _dot_nt(w.astype(BF16), vt_ref[rows_of(h), keys])

    def step(jj, state):
        carry, totals = state
        j = i - 1 - jj
        stage_b(j + 1, carry)
        carry = tuple(c + t for c, t in zip(carry, totals))
        return carry, stage_a(j, False)

    zero = jnp.zeros((tq, 1), F32)
    carry, _ = lax.fori_loop(0, i, step, ((zero,) * n_heads, stage_a(i, True)))
    stage_b(0, carry)
    for s in range(n_slabs):
        o_ref[:, s * LANES:(s + 1) * LANES] = jnp.where(
            low, acc_ref[HEADS_PER_SLAB * s], acc_ref[HEADS_PER_SLAB * s + 1]).astype(BF16)


def _attn_prompt(q, kt, vt, bias, tri, tq, n_slabs):
    b, s, _ = q.shape
    width = n_slabs * LANES
    kv_spec = pl.BlockSpec((None, width, s), lambda bi, gi, qi: (bi, gi, 0))
    q_spec = pl.BlockSpec((None, tq, width), lambda bi, gi, qi: (bi, qi, gi))
    return pl.pallas_call(
        functools.partial(_attn_prompt_body, tq=tq, n_slabs=n_slabs),
        grid=(b, D_MODEL // width, s // tq),
        in_specs=[pl.BlockSpec(memory_space=pltpu.SMEM),
                  pl.BlockSpec((tq, tq), lambda bi, gi, qi: (0, 0)), q_spec, kv_spec, kv_spec],
        out_specs=q_spec,
        out_shape=jax.ShapeDtypeStruct((b, s, D_MODEL), BF16),
        scratch_shapes=[pltpu.VMEM((n_slabs * HEADS_PER_SLAB, tq, LANES), F32),
                        pltpu.VMEM((n_slabs * HEADS_PER_SLAB, tq, tq), F32),
                        pltpu.VMEM((n_slabs * HEADS_PER_SLAB, tq, tq), F32)],
        compiler_params=pltpu.CompilerParams(
            dimension_semantics=("arbitrary", "arbitrary", "arbitrary"),
            vmem_limit_bytes=VMEM_LIMIT),
        name="attn_prompt",
    )(bias, tri, q, kt, vt)


def _attn_sample_body(pt_ref, q_ref, kn_ref, vn_ref, bias_ref, tri_ref, *rest, n_q, pps):
    del pt_ref
    kt_refs, vt_refs = rest[:pps], rest[pps:2 * pps]
    o_ref, qbd_ref, kt_scr, vt_scr, w_scr, acc_ref, carry_ref = rest[2 * pps:]
    step = pl.program_id(1)
    n_rows = N_HEADS * n_q
    window = pps * PAGE_SIZE

    @pl.when(step == 0)
    def _():
        q_rep = jnp.concatenate([q_ref[...].astype(F32)] * N_HEADS, axis=0)
        rh = lax.broadcasted_iota(jnp.int32, (n_rows, D_MODEL), 0) // n_q
        ch = lax.broadcasted_iota(jnp.int32, (n_rows, D_MODEL), 1) // HEAD_DIM
        qbd = jnp.where(rh == ch, q_rep, 0.0).astype(BF16)
        qbd_ref[...] = qbd
        pad = jnp.zeros((PAGE_SIZE - n_q, D_MODEL), F32)
        kn = jnp.concatenate([kn_ref[...], pad], axis=0).astype(BF16)
        vn = jnp.concatenate([vn_ref[...], pad], axis=0).astype(BF16)
        z = _dot_nt(qbd, kn) + bias_ref[...]
        key = lax.broadcasted_iota(jnp.int32, (n_rows, PAGE_SIZE), 1)
        qi = lax.broadcasted_iota(jnp.int32, (n_rows, PAGE_SIZE), 0) % n_q
        visible = key < qi
        lk = jnp.where(visible, _neg_softplus(z), 0.0)
        incl = _dot(lk.astype(BF16), tri_ref[0:PAGE_SIZE, 0:PAGE_SIZE])
        w = jnp.where(visible, jnp.exp(z + incl), 0.0)
        acc_ref[...] = _dot(w.astype(BF16), vn)
        carry_ref[...] = jnp.broadcast_to(incl[:, 0:1], carry_ref.shape)

    for i in range(pps):
        lanes = slice(i * PAGE_SIZE, (i + 1) * PAGE_SIZE)
        kt_scr[:, lanes] = kt_refs[i][...].astype(BF16)
        vt_scr[:, lanes] = vt_refs[i][...].astype(BF16)
    z_all = _dot(qbd_ref[...], kt_scr[...]) + bias_ref[...]
    carry = carry_ref[:, 0:1]
    for ch in reversed(range(window // MXU_DIM)):
        lanes = slice(ch * MXU_DIM, (ch + 1) * MXU_DIM)
        z = z_all[:, lanes]
        incl = _dot(_neg_softplus(z).astype(BF16), tri_ref[...])
        w_scr[:, lanes] = jnp.exp(z + incl + carry).astype(BF16)
        carry = carry + incl[:, 0:1]
    acc_ref[...] += _dot_nt(w_scr[...], vt_scr[...])
    carry_ref[...] = jnp.broadcast_to(carry, carry_ref.shape)

    @pl.when(step == pl.num_programs(1) - 1)
    def _():
        rh = lax.broadcasted_iota(jnp.int32, (n_rows, D_MODEL), 0) // n_q
        ch = lax.broadcasted_iota(jnp.int32, (n_rows, D_MODEL), 1) // HEAD_DIM
        own = jnp.where(rh == ch, acc_ref[...], 0.0)
        out = own[0:n_q, :]
        for h in range(1, N_HEADS):
            out = out + own[h * n_q:(h + 1) * n_q, :]
        o_ref[...] = out.astype(BF16)


def _qkv_sample_body(pt_ref, x_ref, gm_ref, wq_ref, wkt_ref, wvt_ref, qg_ref, kgt_ref,
                     qs_ref, kn_ref, vn_ref, bias_ref, tri_ref, *rest, n_q, pps):
    pages = rest[:2 * pps]
    (q_ref, kt_ref, vt_ref, ktb_ref, vtb_ref, o_ref,
     h_scr, qbd_ref, kt_scr, vt_scr, w_scr, acc_ref, carry_ref) = rest[2 * pps:]
    _attn_sample_body(pt_ref, qs_ref, kn_ref, vn_ref, bias_ref, tri_ref, *pages,
                      o_ref, qbd_ref, kt_scr, vt_scr, w_scr, acc_ref, carry_ref, n_q=n_q, pps=pps)
    _qkv_t_steps(pl.program_id(1), x_ref, gm_ref, wq_ref, wkt_ref, wvt_ref, qg_ref, kgt_ref,
                 q_ref, kt_ref, vt_ref, ktb_ref, vtb_ref, h_scr)


def _qkv_t_and_sample_attn(x2d, g_mix, wq, wkt, wvt, q_gain, k_gain_col, seq_len, tm,
                           qs, k_new, v_new, bias_rows, tri, cache_kt, cache_vt, page_table, pps):
    t = x2d.shape[0]
    db, n_q, _ = qs.shape
    n_pages = page_table.shape[1]
    n_rows = N_HEADS * n_q
    window = pps * PAGE_SIZE
    n_tiles, n_windows = t // tm, n_pages // pps
    tiles_per_seq = seq_len // tm
    assert n_tiles == db and n_pages % pps == 0 and n_windows >= 3

    def page_spec(slot):
        def idx(i, s, pt):
            return (pt[i, n_pages - pps * (s + 1) + slot], 0, 0)
        return pl.BlockSpec((None, D_MODEL, PAGE_SIZE), idx)

    tok = pl.BlockSpec((tm, D_MODEL), lambda i, s, pt: (i, 0))
    feat = pl.BlockSpec((None, D_MODEL, tm), lambda i, s, pt: (i // tiles_per_seq, 0, i % tiles_per_seq))
    per_seq = pl.BlockSpec((None, n_q, D_MODEL), lambda i, s, pt: (i, 0, 0))
    const = lambda i, s, pt: (0, 0)
    wspec = _resident((D_MODEL, D_MODEL))
    pages = [page_spec(slot) for slot in range(pps)]
    shape_t = (t // seq_len, D_MODEL, seq_len)
    grid_spec = pltpu.PrefetchScalarGridSpec(
        num_scalar_prefetch=1,
        grid=(n_tiles, n_windows),
        in_specs=[tok, _resident((1, D_MODEL)), wspec, wspec, wspec,
                  _resident((1, D_MODEL)), _resident((HEAD_DIM, 1)),
                  per_seq, per_seq, per_seq,
                  pl.BlockSpec((n_rows, 1), const),
                  pl.BlockSpec((MXU_DIM, MXU_DIM), const)] + pages + pages,
        out_specs=[tok, feat, feat, feat, feat, per_seq],
        scratch_shapes=[pltpu.VMEM((tm, D_MODEL), BF16),
                        pltpu.VMEM((n_rows, D_MODEL), BF16),
                        pltpu.VMEM((D_MODEL, window), BF16),
                        pltpu.VMEM((D_MODEL, window), BF16),
                        pltpu.VMEM((n_rows, window), BF16),
                        pltpu.VMEM((n_rows, D_MODEL), F32),
                        pltpu.VMEM((n_rows, LANES), F32)])
    return pl.pallas_call(
        functools.partial(_qkv_sample_body, n_q=n_q, pps=pps),
        grid_spec=grid_spec,
        out_shape=[jax.ShapeDtypeStruct((t, D_MODEL), BF16),
                   jax.ShapeDtypeStruct(shape_t, F32),
                   jax.ShapeDtypeStruct(shape_t, F32),
                   jax.ShapeDtypeStruct(shape_t, BF16),
                   jax.ShapeDtypeStruct(shape_t, BF16),
                   jax.ShapeDtypeStruct((db, n_q, D_MODEL), BF16)],
        compiler_params=pltpu.CompilerParams(
            dimension_semantics=("arbitrary", "arbitrary"), vmem_limit_bytes=VMEM_LIMIT),
        name="qkv_t_and_attn_sample",
    )(page_table, x2d, g_mix, wq, wkt, wvt, q_gain, k_gain_col, qs, k_new, v_new, bias_rows, tri,
      *([cache_kt] * pps), *([cache_vt] * pps))


def _ffn_body(*refs, with_wo, short_seq, tiles_per_seq, seq_len):
    refs = list(refs)
    x_ref = refs.pop(0)
    if with_wo:
        o_ref, wo_ref = refs.pop(0), refs.pop(0)
    g_ref, wup_ref, cw_ref, cb_ref, wdn_ref = [refs.pop(0) for _ in range(5)]
    if short_seq:
        p1_ref, p2_ref = refs.pop(0), refs.pop(0)
    out_ref, conv_ref, a_scr, g_scr = refs
    tm = x_ref.shape[0]

    x1 = x_ref[...]
    if with_wo:
        x1 = x1 + _dot(o_ref[...], wo_ref[...])
    h = _rms(x1, g_ref[...]).astype(BF16)

    if short_seq:
        a_scr[0:SUBLANES, :] = jnp.zeros((SUBLANES, D_FF), F32)
    else:
        @pl.when(pl.program_id(0) % tiles_per_seq == 0)
        def _():
            a_scr[0:SUBLANES, :] = jnp.zeros((SUBLANES, D_FF), F32)

    a_scr[SUBLANES:SUBLANES + tm, :] = _dot(h, wup_ref[:, 0:D_FF])

    row = lax.broadcasted_iota(jnp.int32, (tm, FF_CHUNK), 0)
    for c in range(D_FF // FF_CHUNK):
        cs = slice(c * FF_CHUNK, (c + 1) * FF_CHUNK)
        a0 = a_scr[SUBLANES:SUBLANES + tm, cs]
        a1 = a_scr[SUBLANES - 1:SUBLANES - 1 + tm, cs]
        a2 = a_scr[SUBLANES - 2:SUBLANES - 2 + tm, cs]
        if short_seq:
            pos = row % seq_len
            a1 = jnp.where(pos < 1, p1_ref[:, cs], a1)
            a2 = jnp.where(pos < 2, p2_ref[:, cs], a2)
        b = _dot(h, wup_ref[:, D_FF + c * FF_CHUNK:D_FF + (c + 1) * FF_CHUNK])
        cc = cb_ref[:, cs] + cw_ref[0:1, cs] * a2 + cw_ref[1:2, cs] * a1 + cw_ref[2:3, cs] * a0
        g_scr[:, cs] = (cc * jax.nn.sigmoid(cc) * b).astype(BF16)

    tail = a_scr[tm:tm + SUBLANES, :]
    if short_seq:
        conv_ref[...] = a_scr[SUBLANES:SUBLANES + tm, :]
    else:
        conv_ref[...] = tail
        a_scr[0:SUBLANES, :] = tail
    out_ref[...] = x1 + _dot(g_scr[...], wdn_ref[...])


def _layer_resident(shape, layer):
    nd = len(shape)
    return pl.BlockSpec((None,) + tuple(shape), lambda *_: (layer,) + (0,) * nd,
                        pipeline_mode=pl.Buffered(1))


def _ffn(x2d, o2d, w_o, g_ffn, w_up, conv_w, conv_b, w_down, layer, seq_len, tm, prev_rows=None):
    t = x2d.shape[0]
    with_wo = o2d is not None
    short_seq = prev_rows is not None
    n_tiles = t // tm
    row = lambda i: (i, 0)
    tok = pl.BlockSpec((tm, D_MODEL), row)
    args, specs = [x2d], [tok]
    if with_wo:
        args += [o2d, w_o]
        specs += [tok, _resident((D_MODEL, D_MODEL))]
    args += [g_ffn, w_up, conv_w, conv_b, w_down]
    specs += [_layer_resident((1, D_MODEL), layer), _layer_resident((D_MODEL, 2 * D_FF), layer),
              _layer_resident((3, D_FF), layer), _layer_resident((1, D_FF), layer),
              _layer_resident((D_FF, D_MODEL), layer)]
    if short_seq:
        args += list(prev_rows)
        specs += [pl.BlockSpec((tm, D_FF), row)] * 2
        conv_shape, conv_spec = (t, D_FF), pl.BlockSpec((tm, D_FF), row)
        tiles_per_seq = 1
    else:
        conv_shape = (n_tiles, SUBLANES, D_FF)
        conv_spec = pl.BlockSpec((None, SUBLANES, D_FF), lambda i: (i, 0, 0))
        tiles_per_seq = seq_len // tm
    return pl.pallas_call(
        functools.partial(_ffn_body, with_wo=with_wo, short_seq=short_seq,
                          tiles_per_seq=tiles_per_seq, seq_len=seq_len),
        grid=(n_tiles,),
        in_specs=specs,
        out_specs=[tok, conv_spec],
        out_shape=[jax.ShapeDtypeStruct((t, D_MODEL), F32),
                   jax.ShapeDtypeStruct(conv_shape, F32)],
        scratch_shapes=[pltpu.VMEM((tm + SUBLANES, D_FF), F32), pltpu.VMEM((tm, D_FF), BF16)],
        compiler_params=pltpu.CompilerParams(
            dimension_semantics=("arbitrary",), vmem_limit_bytes=VMEM_LIMIT),
        name="conv_ffn",
    )(*args)


def _s5_body(x_ref, g_ref, win_ref, bd_ref, cd_ref, are_ref, aim_ref, dskip_ref, wglu_ref, s0_ref,
             xo_ref, sfin_ref, xt_scr, st_scr, bu_scr, y_scr, *, tt):
    ti = pl.program_id(1)
    n_seq = SUBLANES

    @pl.when(ti == 0)
    def _():
        st_scr[...] = s0_ref[...]

    n_slabs = D_MODEL // LANES
    for b in range(n_seq):
        for j in range(n_slabs):
            xt_scr[j, pl.ds(b, tt, stride=n_seq), :] = x_ref[b, :, j * LANES:(j + 1) * LANES]
    xt = jnp.concatenate([xt_scr[j] for j in range(n_slabs)], axis=1)
    u = _dot(_rms(xt, g_ref[...]).astype(BF16), win_ref[...])
    ub = u.astype(BF16)

    for k in range(N_SSM_BLOCKS):
        bu_scr[...] = _dot(ub[:, k * SSM_BLOCK_IN:(k + 1) * SSM_BLOCK_IN], bd_ref[k])
        for lc in range(SSM_BLOCK_STATE // SSM_SCAN_LANES):
            re_l = slice(lc * SSM_SCAN_LANES, (lc + 1) * SSM_SCAN_LANES)
            im_l = slice(SSM_BLOCK_STATE + lc * SSM_SCAN_LANES,
                         SSM_BLOCK_STATE + (lc + 1) * SSM_SCAN_LANES)
            base = k * 2 * SSM_BLOCK_STATE
            st_re = slice(base + re_l.start, base + re_l.stop)
            st_im = slice(base + im_l.start, base + im_l.stop)
            ar = jnp.broadcast_to(are_ref[k, :, re_l], (n_seq, SSM_SCAN_LANES))
            ai = jnp.broadcast_to(aim_ref[k, :, re_l], (n_seq, SSM_SCAN_LANES))

            def step(t, s, ar=ar, ai=ai, re_l=re_l, im_l=im_l):
                sr, si = s
                r0 = pl.multiple_of(t * n_seq, n_seq)
                nr = ar * sr - ai * si + bu_scr[pl.ds(r0, n_seq), re_l]
                ni = ar * si + ai * sr + bu_scr[pl.ds(r0, n_seq), im_l]
                bu_scr[pl.ds(r0, n_seq), re_l] = nr
                bu_scr[pl.ds(r0, n_seq), im_l] = ni
                return nr, ni

            sr, si = lax.fori_loop(0, tt, step, (st_scr[:, st_re], st_scr[:, st_im]), unroll=True)
            st_scr[:, st_re] = sr
            st_scr[:, st_im] = si
        y_scr[:, k * SSM_BLOCK_IN:(k + 1) * SSM_BLOCK_IN] = _dot(bu_scr[...].astype(BF16), cd_ref[k])

    y = y_scr[...] + dskip_ref[...] * u
    gg = _dot(jax.nn.gelu(y).astype(BF16), wglu_ref[...])
    x_new = xt + gg[:, 0:D_MODEL] * jax.nn.sigmoid(gg[:, D_MODEL:2 * D_MODEL])
    for j in range(n_slabs):
        xt_scr[j] = x_new[:, j * LANES:(j + 1) * LANES]
    for b in range(n_seq):
        for j in range(n_slabs):
            xo_ref[b, :, j * LANES:(j + 1) * LANES] = xt_scr[j, pl.ds(b, tt, stride=n_seq), :]

    @pl.when(ti == pl.num_programs(1) - 1)
    def _():
        sfin_ref[...] = st_scr[...]


def _s5(x, g_mix, w_in, bd, cd, a_re, a_im, d_skip, w_glu, s0, tt):
    b, s, _ = x.shape
    n_state = N_SSM_BLOCKS * 2 * SSM_BLOCK_STATE
    x_spec = pl.BlockSpec((SUBLANES, tt, D_MODEL), lambda bi, ti: (bi, ti, 0))
    s_spec = pl.BlockSpec((None, SUBLANES, n_state), lambda bi, ti: (bi, 0, 0))
    rows = tt * SUBLANES
    return pl.pallas_call(
        functools.partial(_s5_body, tt=tt),
        grid=(b // SUBLANES, s // tt),
        in_specs=[x_spec, _resident((1, D_MODEL)), _resident((D_MODEL, D_MODEL)),
                  _resident(bd.shape), _resident(cd.shape), _resident(a_re.shape),
                  _resident(a_im.shape), _resident((1, D_MODEL)),
                  _resident((D_MODEL, 2 * D_MODEL)), s_spec],
        out_specs=[x_spec, s_spec],
        out_shape=[jax.ShapeDtypeStruct((b, s, D_MODEL), F32),
                   jax.ShapeDtypeStruct((b // SUBLANES, SUBLANES, n_state), F32)],
        scratch_shapes=[pltpu.VMEM((D_MODEL // LANES, rows, LANES), F32),
                        pltpu.VMEM((SUBLANES, n_state), F32),
                        pltpu.VMEM((rows, 2 * SSM_BLOCK_STATE), F32),
                        pltpu.VMEM((rows, D_MODEL), F32)],
        compiler_params=pltpu.CompilerParams(
            dimension_semantics=("arbitrary", "arbitrary"), vmem_limit_bytes=VMEM_LIMIT),
        name="s5_layer",
    )(x, g_mix, w_in, bd, cd, a_re, a_im, d_skip, w_glu, s0)


def _s5_tables(lam_re, lam_im, log_dt, b_re, b_im, c_re, c_im):
    dt = jnp.exp(log_dt)[:, None]
    mag = jnp.exp(lam_re * dt)
    a_re = mag * jnp.cos(lam_im * dt)
    a_im = mag * jnp.sin(lam_im * dt)
    den = lam_re * lam_re + lam_im * lam_im
    zr = ((a_re - 1.0) * lam_re + a_im * lam_im) / den
    zi = (a_im * lam_re - (a_re - 1.0) * lam_im) / den
    bbar_re = zr[..., None] * b_re - zi[..., None] * b_im
    bbar_im = zr[..., None] * b_im + zi[..., None] * b_re
    eye = jnp.eye(SSM_GROUPS_PER_BLOCK, dtype=F32)

    def pack_in(m):
        m = m.reshape(N_SSM_BLOCKS, SSM_GROUPS_PER_BLOCK, STATE_DIM, SSM_GROUP)
        full = jnp.einsum("kgpc,gh->kgchp", m, eye)
        return full.reshape(N_SSM_BLOCKS, SSM_BLOCK_IN, SSM_BLOCK_STATE)

    def pack_out(m):
        m = m.reshape(N_SSM_BLOCKS, SSM_GROUPS_PER_BLOCK, SSM_GROUP, STATE_DIM)
        full = jnp.einsum("kgcp,gh->kgphc", m, eye)
        return full.reshape(N_SSM_BLOCKS, SSM_BLOCK_STATE, SSM_BLOCK_IN)

    bd = jnp.concatenate([pack_in(bbar_re), pack_in(bbar_im)], axis=2).astype(BF16)
    cd = jnp.concatenate([pack_out(c_re), pack_out(-c_im)], axis=1).astype(BF16)
    shape = (N_SSM_BLOCKS, 1, SSM_BLOCK_STATE)
    return bd, cd, a_re.reshape(shape), a_im.reshape(shape)


def _pack_state(s_re, s_im):
    n = s_re.shape[0]
    packed = jnp.concatenate([s_re.reshape(n, N_SSM_BLOCKS, SSM_BLOCK_STATE),
                              s_im.reshape(n, N_SSM_BLOCKS, SSM_BLOCK_STATE)], axis=2)
    return packed.reshape(n // SUBLANES, SUBLANES, -1)


def _unpack_state(packed, n):
    s = packed.reshape(n, N_SSM_BLOCKS, 2, SSM_BLOCK_STATE)
    return (s[:, :, 0].reshape(n, N_GROUPS, STATE_DIM), s[:, :, 1].reshape(n, N_GROUPS, STATE_DIM))


def _attention_mixers(x_p, x_s, w, cache_kt, cache_vt, page_table, *, tm_p, tm_s, tq, n_slabs, pps):
    b, s, _ = x_p.shape
    db, ds, _ = x_s.shape
    shape_s = (db, ds, D_MODEL)
    q_s, k_s, v_s = _qkv(x_s.reshape(db * ds, D_MODEL), w["norm_mix"][0], w["wq"], w["wk"], w["wv"],
                         w["q_gain"], w["k_gain"], tm_s)
    q_p, kt, vt, ktb, vtb, o_s = _qkv_t_and_sample_attn(
        x_p.reshape(b * s, D_MODEL), w["norm_mix"][0], w["wq"], w["wkt"], w["wvt"], w["q_gain"],
        w["k_gain_col"], s, tm_p, q_s.reshape(shape_s), k_s.reshape(shape_s), v_s.reshape(shape_s),
        w["bias_rows"], w["tri"], cache_kt, cache_vt, page_table, pps)
    o_p = _attn_prompt(q_p.reshape(b, s, D_MODEL), ktb, vtb, w["bias"] * LOG2_E, -w["tri"], tq, n_slabs)
    feat_major = lambda a: jnp.transpose(a.reshape(b, N_HEADS, HEAD_DIM, s), (0, 3, 1, 2))
    heads_s = (db, ds, N_HEADS, HEAD_DIM)
    return (o_p, feat_major(kt), feat_major(vt)), (o_s, k_s.reshape(heads_s), v_s.reshape(heads_s))


def _after_attention(x, o, w, tables, *, tm, tt, ssm_prev=None, conv_prev=None):
    b, s, _ = x.shape
    t = b * s
    x2d = x.reshape(t, D_MODEL)
    shape3 = (b, s, D_MODEL)
    prev0 = None if conv_prev is None else conv_prev[0]
    x2d, conv0 = _ffn(x2d, o.reshape(t, D_MODEL), w["w_o"], w["norm_ffn"], w["w_up"],
                      w["conv_w"], w["conv_b"], w["w_down"], 0, s, tm, prev0)

    bd, cd, a_re, a_im = tables
    if ssm_prev is None:
        s0 = jnp.zeros((b // SUBLANES, SUBLANES, N_SSM_BLOCKS * 2 * SSM_BLOCK_STATE), F32)
    else:
        s0 = _pack_state(*ssm_prev)
    x3, s_fin = _s5(x2d.reshape(shape3), w["norm_mix"][1], w["w_in"], bd, cd, a_re, a_im,
                    w["d_skip"], w["w_glu"], s0, tt)
    s_re, s_im = _unpack_state(s_fin, b)
    prev1 = None if conv_prev is None else conv_prev[1]
    y2d, conv1 = _ffn(x3.reshape(t, D_MODEL), None, None, w["norm_ffn"], w["w_up"],
                      w["conv_w"], w["conv_b"], w["w_down"], 1, s, tm, prev1)

    def conv_state(c):
        if conv_prev is None:
            per_seq = c.reshape(b, s // tm, SUBLANES, D_FF)[:, -1]
        else:
            per_seq = c.reshape(b, s, D_FF)
        return per_seq[:, -2:]

    return (y2d.reshape(shape3), s_re[None], s_im[None],
            jnp.stack([conv_state(conv0), conv_state(conv1)]))


def _short_seq_prev_rows(prev, seq_len):
    bsz = prev.shape[0]
    zeros = jnp.zeros((bsz, seq_len - 2, D_FF), F32)
    p2 = jnp.concatenate([prev, zeros], axis=1)
    p1 = jnp.concatenate([prev[:, 1:2], jnp.zeros((bsz, seq_len - 1, D_FF), F32)], axis=1)
    return p1.reshape(bsz * seq_len, D_FF), p2.reshape(bsz * seq_len, D_FF)


def kernel(x_prompt, x_sample, cache_k, cache_v, state_ssm_re, state_ssm_im, state_ffn_conv, page_table,
           norm_mix, norm_ffn, attn_w_qkv, attn_q_gain, attn_k_gain, attn_logit_bias, attn_w_o,
           ssm_w_in, ssm_lambda_re, ssm_lambda_im, ssm_log_dt, ssm_b_re, ssm_b_im, ssm_c_re, ssm_c_im,
           ssm_d, ssm_w_glu, ffn_w_up, ffn_conv_w, ffn_conv_b, ffn_w_down):
    dec_batch, dec_seq, _ = x_sample.shape
    w = {
        "norm_mix": norm_mix[:, None, :], "norm_ffn": norm_ffn[:, None, :],
        "wq": attn_w_qkv[0, :, 0:D_MODEL].astype(BF16),
        "wk": attn_w_qkv[0, :, D_MODEL:2 * D_MODEL].astype(BF16),
        "wv": attn_w_qkv[0, :, 2 * D_MODEL:3 * D_MODEL].astype(BF16),
        "wkt": attn_w_qkv[0, :, D_MODEL:2 * D_MODEL].T.astype(BF16),
        "wvt": attn_w_qkv[0, :, 2 * D_MODEL:3 * D_MODEL].T.astype(BF16),
        "q_gain": jnp.tile(attn_q_gain[0], N_HEADS)[None], "k_gain": jnp.tile(attn_k_gain[0], N_HEADS)[None],
        "k_gain_col": attn_k_gain[0][:, None],
        "bias": attn_logit_bias[0],
        "bias_rows": jnp.repeat(attn_logit_bias[0], dec_seq)[:, None],
        "tri": jnp.tril(jnp.ones((MXU_DIM, MXU_DIM), BF16)),
        "w_o": attn_w_o[0].astype(BF16),
        "w_in": ssm_w_in[0].astype(BF16), "d_skip": ssm_d[0][None], "w_glu": ssm_w_glu[0].astype(BF16),
        "w_up": ffn_w_up.astype(BF16), "conv_w": ffn_conv_w, "conv_b": ffn_conv_b[:, None, :],
        "w_down": ffn_w_down.astype(BF16),
    }
    tables = _s5_tables(ssm_lambda_re[0], ssm_lambda_im[0], ssm_log_dt[0], ssm_b_re[0], ssm_b_im[0],
                        ssm_c_re[0], ssm_c_im[0])

    n_pool = cache_k.shape[1]
    to_pages = lambda c: jnp.transpose(c[0], (0, 2, 3, 1)).reshape(n_pool, D_MODEL, PAGE_SIZE)
    tm_s = dec_batch * dec_seq
    (o_p, k_p, v_p), (o_s, k_s, v_s) = _attention_mixers(
        x_prompt, x_sample, w, to_pages(cache_k), to_pages(cache_v), page_table,
        tm_p=512, tm_s=tm_s, tq=256, n_slabs=8, pps=8)

    y_p, sr_p, si_p, conv_p = _after_attention(x_prompt, o_p, w, tables, tm=512, tt=64)
    conv_prev = [_short_seq_prev_rows(state_ffn_conv[i], dec_seq) for i in range(state_ffn_conv.shape[0])]
    y_s, sr_s, si_s, conv_s = _after_attention(
        x_sample, o_s, w, tables, tm=tm_s, tt=dec_seq,
        ssm_prev=(state_ssm_re[0], state_ssm_im[0]), conv_prev=conv_prev)
    k_p, v_p, k_s, v_s = k_p[None], v_p[None], k_s[None], v_s[None]

    return (y_p, y_s, k_p, v_p, k_s, v_s, sr_p, si_p, sr_s, si_s, conv_p, conv_s)
```

```python
import functools
import math

import jax
import jax.numpy as jnp
from jax import lax
from jax.experimental import pallas as pl
from jax.experimental.pallas import tpu as pltpu

F32 = jnp.float32
BF16 = jnp.bfloat16

D_MODEL = 1024
N_HEADS = 16
HEAD_DIM = 64
PAGE_SIZE = 128
SSM_GROUP = 16
N_GROUPS = D_MODEL // SSM_GROUP
STATE_DIM = 64
D_FF = 2816
NORM_EPS = 1e-6

LANES = 128
SUBLANES = 8
MXU_DIM = 256
VMEM_LIMIT = 56 * 1024 * 1024

HEADS_PER_SLAB = LANES // HEAD_DIM
SSM_BLOCK_IN = MXU_DIM
SSM_GROUPS_PER_BLOCK = SSM_BLOCK_IN // SSM_GROUP
SSM_BLOCK_STATE = SSM_GROUPS_PER_BLOCK * STATE_DIM
N_SSM_BLOCKS = D_MODEL // SSM_BLOCK_IN
SSM_SCAN_LANES = 512
S5_PARTS = 2
FF_CHUNK = MXU_DIM
LOG2_E = math.log2(math.e)
MASKED_LOGIT = -1e30


def _dot(a, b):
    return jnp.dot(a, b, preferred_element_type=F32)


def _dot_nt(a, b):
    return lax.dot_general(a, b, (((1,), (1,)), ((), ())), preferred_element_type=F32)


def _rms(x, gain):
    return x * lax.rsqrt(jnp.mean(x * x, axis=-1, keepdims=True) + NORM_EPS) * gain


def _resident(shape):
    nd = len(shape)
    return pl.BlockSpec(shape, lambda *_: (0,) * nd, pipeline_mode=pl.Buffered(1))


def _neg_softplus(z):
    return -(jnp.maximum(z, 0.0) + jnp.log(1.0 + jnp.exp(-jnp.abs(z))))


def _head_norm_rows(t, gain):
    r = lax.broadcasted_iota(jnp.int32, (MXU_DIM, MXU_DIM), 0) // HEAD_DIM
    c = lax.broadcasted_iota(jnp.int32, (MXU_DIM, MXU_DIM), 1) // HEAD_DIM
    seg = (r == c).astype(BF16)
    sq = (t * t).astype(BF16)
    ms = jnp.concatenate(
        [_dot(sq[:, j * MXU_DIM:(j + 1) * MXU_DIM], seg) for j in range(D_MODEL // MXU_DIM)],
        axis=1) * (1.0 / HEAD_DIM)
    return t * lax.rsqrt(ms + NORM_EPS) * gain


def _qkv_body(x_ref, gm_ref, wq_ref, wk_ref, wv_ref, qg_ref, kg_ref, q_ref, k_ref, v_ref):
    h = _rms(x_ref[...], gm_ref[...]).astype(BF16)
    q = _head_norm_rows(_dot(h, wq_ref[...]), qg_ref[...])
    q_ref[...] = (q * (HEAD_DIM ** -0.5)).astype(BF16)
    k_ref[...] = _head_norm_rows(_dot(h, wk_ref[...]), kg_ref[...])
    v_ref[...] = _dot(h, wv_ref[...])


def _qkv(x2d, g_mix, wq, wk, wv, q_gain, k_gain, tm):
    t = x2d.shape[0]
    tok = pl.BlockSpec((tm, D_MODEL), lambda i: (i, 0))
    wspec = _resident((D_MODEL, D_MODEL))
    return pl.pallas_call(
        _qkv_body,
        grid=(t // tm,),
        in_specs=[tok, _resident((1, D_MODEL)), wspec, wspec, wspec,
                  _resident((1, D_MODEL)), _resident((1, D_MODEL))],
        out_specs=[tok, tok, tok],
        out_shape=[jax.ShapeDtypeStruct((t, D_MODEL), BF16),
                   jax.ShapeDtypeStruct((t, D_MODEL), F32),
                   jax.ShapeDtypeStruct((t, D_MODEL), F32)],
        compiler_params=pltpu.CompilerParams(
            dimension_semantics=("arbitrary",), vmem_limit_bytes=VMEM_LIMIT),
        name="qkv_proj",
    )(x2d, g_mix, wq, wk, wv, q_gain, k_gain)


def _qkv_t_body(x_ref, gm_ref, wq_ref, wkt_ref, wvt_ref, qg_ref, kgt_ref,
                q_ref, kt_ref, vt_ref, ktb_ref, vtb_ref):
    tm = x_ref.shape[0]
    h = _rms(x_ref[...], gm_ref[...]).astype(BF16)
    q = _head_norm_rows(_dot(h, wq_ref[...]), qg_ref[...])
    q_ref[...] = (q * (HEAD_DIM ** -0.5 * LOG2_E)).astype(BF16)
    k3 = _dot_nt(wkt_ref[...], h).reshape(N_HEADS, HEAD_DIM, tm)
    ms = jnp.mean(k3 * k3, axis=1, keepdims=True)
    kt = (k3 * lax.rsqrt(ms + NORM_EPS) * kgt_ref[...][None]).reshape(D_MODEL, tm)
    kt_ref[...] = kt
    ktb_ref[...] = kt.astype(BF16)
    vt = _dot_nt(wvt_ref[...], h)
    vt_ref[...] = vt
    vtb_ref[...] = vt.astype(BF16)


def _qkv_t(x2d, g_mix, wq, wkt, wvt, q_gain, k_gain_col, seq_len, tm):
    t = x2d.shape[0]
    tiles_per_seq = seq_len // tm
    tok = pl.BlockSpec((tm, D_MODEL), lambda i: (i, 0))
    feat = pl.BlockSpec((None, D_MODEL, tm), lambda i: (i // tiles_per_seq, 0, i % tiles_per_seq))
    wspec = _resident((D_MODEL, D_MODEL))
    shape_t = (t // seq_len, D_MODEL, seq_len)
    return pl.pallas_call(
        _qkv_t_body,
        grid=(t // tm,),
        in_specs=[tok, _resident((1, D_MODEL)), wspec, wspec, wspec,
                  _resident((1, D_MODEL)), _resident((HEAD_DIM, 1))],
        out_specs=[tok, feat, feat, feat, feat],
        out_shape=[jax.ShapeDtypeStruct((t, D_MODEL), BF16),
                   jax.ShapeDtypeStruct(shape_t, F32),
                   jax.ShapeDtypeStruct(shape_t, F32),
                   jax.ShapeDtypeStruct(shape_t, BF16),
                   jax.ShapeDtypeStruct(shape_t, BF16)],
        compiler_params=pltpu.CompilerParams(
            dimension_semantics=("arbitrary",), vmem_limit_bytes=VMEM_LIMIT),
        name="qkv_proj_t",
    )(x2d, g_mix, wq, wkt, wvt, q_gain, k_gain_col)


def _attn_prompt_body(bias_ref, tri_ref, q_ref, kt_ref, vt_ref, o_ref, acc_ref, z_scr, i_scr,
                      *, tq, n_slabs):
    grp = pl.program_id(1)
    i = pl.program_id(2)
    n_heads = n_slabs * HEADS_PER_SLAB
    low = lax.broadcasted_iota(jnp.int32, (tq, LANES), 1) < HEAD_DIM
    q_heads = []
    for s in range(n_slabs):
        q = q_ref[:, s * LANES:(s + 1) * LANES]
        q_heads += [jnp.where(low, q, jnp.zeros_like(q)), jnp.where(low, jnp.zeros_like(q), q)]
    bias = [bias_ref[grp * n_heads + h] for h in range(n_heads)]
    row = lax.broadcasted_iota(jnp.int32, (tq, tq), 0)
    col = lax.broadcasted_iota(jnp.int32, (tq, tq), 1)
    causal = col < row

    acc_ref[...] = jnp.zeros_like(acc_ref)

    def rows_of(h):
        return slice((h // HEADS_PER_SLAB) * LANES, (h // HEADS_PER_SLAB + 1) * LANES)

    def stage_a(j, diag):
        keys = pl.ds(pl.multiple_of(j * tq, tq), tq)
        totals = []
        for h in range(n_heads):
            z = _dot(q_heads[h], kt_ref[rows_of(h), keys]) + bias[h]
            if diag:
                z = jnp.where(causal, z, MASKED_LOGIT)
            sp = jnp.maximum(z, 0.0) + jnp.log2(1.0 + jnp.exp2(-jnp.abs(z)))
            incl = _dot(sp.astype(BF16), tri_ref[...])
            z_scr[h] = z
            i_scr[h] = incl
            totals.append(incl[:, 0:1])
        return tuple(totals)

    def stage_b(j, carry):
        keys = pl.ds(pl.multiple_of(j * tq, tq), tq)
        for h in range(n_heads):
            w = jnp.exp2(z_scr[h] + i_scr[h] + carry[h])
            acc_ref[h] += _dot_nt(w.astype(BF16), vt_ref[rows_of(h), keys])

    def step(jj, state):
        carry, totals = state
        j = i - 1 - jj
        stage_b(j + 1, carry)
        carry = tuple(c + t for c, t in zip(carry, totals))
        return carry, stage_a(j, False)

    zero = jnp.zeros((tq, 1), F32)
    carry, _ = lax.fori_loop(0, i, step, ((zero,) * n_heads, stage_a(i, True)))
    stage_b(0, carry)
    for s in range(n_slabs):
        o_ref[:, s * LANES:(s + 1) * LANES] = jnp.where(
            low, acc_ref[HEADS_PER_SLAB * s], acc_ref[HEADS_PER_SLAB * s + 1]).astype(BF16)


def _attn_prompt(q, kt, vt, bias, tri, tq, n_slabs):
    b, s, _ = q.shape
    width = n_slabs * LANES
    kv_spec = pl.BlockSpec((None, width, s), lambda bi, gi, qi: (bi, gi, 0))
    q_spec = pl.BlockSpec((None, tq, width), lambda bi, gi, qi: (bi, qi, gi))
    return pl.pallas_call(
        functools.partial(_attn_prompt_body, tq=tq, n_slabs=n_slabs),
        grid=(b, D_MODEL // width, s // tq),
        in_specs=[pl.BlockSpec(memory_space=pltpu.SMEM),
                  pl.BlockSpec((tq, tq), lambda bi, gi, qi: (0, 0)), q_spec, kv_spec, kv_spec],
        out_specs=q_spec,
        out_shape=jax.ShapeDtypeStruct((b, s, D_MODEL), BF16),
        scratch_shapes=[pltpu.VMEM((n_slabs * HEADS_PER_SLAB, tq, LANES), F32),
                        pltpu.VMEM((n_slabs * HEADS_PER_SLAB, tq, tq), F32),
                        pltpu.VMEM((n_slabs * HEADS_PER_SLAB, tq, tq), F32)],
        compiler_params=pltpu.CompilerParams(
            dimension_semantics=("arbitrary", "arbitrary", "arbitrary"),
            vmem_limit_bytes=VMEM_LIMIT),
        name="attn_prompt",
    )(bias, tri, q, kt, vt)


def _attn_sample_body(pt_ref, q_ref, kn_ref, vn_ref, bias_ref, tri_ref, *rest, n_q, pps):
    del pt_ref
    kt_refs, vt_refs = rest[:pps], rest[pps:2 * pps]
    o_ref, qbd_ref, kt_scr, vt_scr, w_scr, acc_ref, carry_ref = rest[2 * pps:]
    step = pl.program_id(1)
    n_rows = N_HEADS * n_q
    window = pps * PAGE_SIZE

    @pl.when(step == 0)
    def _():
        q_rep = jnp.concatenate([q_ref[...].astype(F32)] * N_HEADS, axis=0)
        rh = lax.broadcasted_iota(jnp.int32, (n_rows, D_MODEL), 0) // n_q
        ch = lax.broadcasted_iota(jnp.int32, (n_rows, D_MODEL), 1) // HEAD_DIM
        qbd = jnp.where(rh == ch, q_rep, 0.0).astype(BF16)
        qbd_ref[...] = qbd
        pad = jnp.zeros((PAGE_SIZE - n_q, D_MODEL), F32)
        kn = jnp.concatenate([kn_ref[...], pad], axis=0).astype(BF16)
        vn = jnp.concatenate([vn_ref[...], pad], axis=0).astype(BF16)
        z = _dot_nt(qbd, kn) + bias_ref[...]
        key = lax.broadcasted_iota(jnp.int32, (n_rows, PAGE_SIZE), 1)
        qi = lax.broadcasted_iota(jnp.int32, (n_rows, PAGE_SIZE), 0) % n_q
        visible = key < qi
        lk = jnp.where(visible, _neg_softplus(z), 0.0)
        incl = _dot(lk.astype(BF16), tri_ref[0:PAGE_SIZE, 0:PAGE_SIZE])
        w = jnp.where(visible, jnp.exp(z + incl), 0.0)
        acc_ref[...] = _dot(w.astype(BF16), vn)
        carry_ref[...] = jnp.broadcast_to(incl[:, 0:1], carry_ref.shape)

    for i in range(pps):
        lanes = slice(i * PAGE_SIZE, (i + 1) * PAGE_SIZE)
        kt_scr[:, lanes] = kt_refs[i][...].astype(BF16)
        vt_scr[:, lanes] = vt_refs[i][...].astype(BF16)
    z_all = _dot(qbd_ref[...], kt_scr[...]) + bias_ref[...]
    carry = carry_ref[:, 0:1]
    for ch in reversed(range(window // MXU_DIM)):
        lanes = slice(ch * MXU_DIM, (ch + 1) * MXU_DIM)
        z = z_all[:, lanes]
        incl = _dot(_neg_softplus(z).astype(BF16), tri_ref[...])
        w_scr[:, lanes] = jnp.exp(z + incl + carry).astype(BF16)
        carry = carry + incl[:, 0:1]
    acc_ref[...] += _dot_nt(w_scr[...], vt_scr[...])
    carry_ref[...] = jnp.broadcast_to(carry, carry_ref.shape)

    @pl.when(step == pl.num_programs(1) - 1)
    def _():
        rh = lax.broadcasted_iota(jnp.int32, (n_rows, D_MODEL), 0) // n_q
        ch = lax.broadcasted_iota(jnp.int32, (n_rows, D_MODEL), 1) // HEAD_DIM
        own = jnp.where(rh == ch, acc_ref[...], 0.0)
        out = own[0:n_q, :]
        for h in range(1, N_HEADS):
            out = out + own[h * n_q:(h + 1) * n_q, :]
        o_ref[...] = out.astype(BF16)


def _attn_sample(q, k_new, v_new, bias_rows, tri, cache_kt, cache_vt, page_table, pps):
    db, n_q, _ = q.shape
    n_pages = page_table.shape[1]
    n_rows = N_HEADS * n_q
    window = pps * PAGE_SIZE

    def page_spec(i):
        def idx(bi, si, pt):
            return (pt[bi, n_pages - pps * (si + 1) + i], 0, 0)
        return pl.BlockSpec((None, D_MODEL, PAGE_SIZE), idx)

    per_seq = lambda bi, si, pt: (bi, 0, 0)
    const = lambda bi, si, pt: (0, 0)
    pages = [page_spec(i) for i in range(pps)]
    grid_spec = pltpu.PrefetchScalarGridSpec(
        num_scalar_prefetch=1,
        grid=(db, n_pages // pps),
        in_specs=[pl.BlockSpec((None, n_q, D_MODEL), per_seq),
                  pl.BlockSpec((None, n_q, D_MODEL), per_seq),
                  pl.BlockSpec((None, n_q, D_MODEL), per_seq),
                  pl.BlockSpec((n_rows, 1), const),
                  pl.BlockSpec((MXU_DIM, MXU_DIM), const)] + pages + pages,
        out_specs=pl.BlockSpec((None, n_q, D_MODEL), per_seq),
        scratch_shapes=[pltpu.VMEM((n_rows, D_MODEL), BF16),
                        pltpu.VMEM((D_MODEL, window), BF16),
                        pltpu.VMEM((D_MODEL, window), BF16),
                        pltpu.VMEM((n_rows, window), BF16),
                        pltpu.VMEM((n_rows, D_MODEL), F32),
                        pltpu.VMEM((n_rows, LANES), F32)])
    return pl.pallas_call(
        functools.partial(_attn_sample_body, n_q=n_q, pps=pps),
        grid_spec=grid_spec,
        out_shape=jax.ShapeDtypeStruct((db, n_q, D_MODEL), BF16),
        compiler_params=pltpu.CompilerParams(
            dimension_semantics=("arbitrary", "arbitrary"), vmem_limit_bytes=VMEM_LIMIT),
        name="attn_sample",
    )(page_table, q, k_new, v_new, bias_rows, tri, *([cache_kt] * pps), *([cache_vt] * pps))


def _ffn_body(*refs, with_wo, short_seq, tiles_per_seq, seq_len):
    refs = list(refs)
    x_ref = refs.pop(0)
    if with_wo:
        o_ref, wo_ref = refs.pop(0), refs.pop(0)
    g_ref, wup_ref, cw_ref, cb_ref, wdn_ref = [refs.pop(0) for _ in range(5)]
    if short_seq:
        p1_ref, p2_ref = refs.pop(0), refs.pop(0)
    out_ref, conv_ref, a_scr, g_scr = refs
    tm = x_ref.shape[0]

    x1 = x_ref[...]
    if with_wo:
        x1 = x1 + _dot(o_ref[...], wo_ref[...])
    h = _rms(x1, g_ref[...]).astype(BF16)

    if short_seq:
        a_scr[0:SUBLANES, :] = jnp.zeros((SUBLANES, D_FF), F32)
    else:
        @pl.when(pl.program_id(0) % tiles_per_seq == 0)
        def _():
            a_scr[0:SUBLANES, :] = jnp.zeros((SUBLANES, D_FF), F32)

    a_scr[SUBLANES:SUBLANES + tm, :] = _dot(h, wup_ref[:, 0:D_FF])

    row = lax.broadcasted_iota(jnp.int32, (tm, FF_CHUNK), 0)
    for c in range(D_FF // FF_CHUNK):
        cs = slice(c * FF_CHUNK, (c + 1) * FF_CHUNK)
        a0 = a_scr[SUBLANES:SUBLANES + tm, cs]
        a1 = a_scr[SUBLANES - 1:SUBLANES - 1 + tm, cs]
        a2 = a_scr[SUBLANES - 2:SUBLANES - 2 + tm, cs]
        if short_seq:
            pos = row % seq_len
            a1 = jnp.where(pos < 1, p1_ref[:, cs], a1)
            a2 = jnp.where(pos < 2, p2_ref[:, cs], a2)
        b = _dot(h, wup_ref[:, D_FF + c * FF_CHUNK:D_FF + (c + 1) * FF_CHUNK])
        cc = cb_ref[:, cs] + cw_ref[0:1, cs] * a2 + cw_ref[1:2, cs] * a1 + cw_ref[2:3, cs] * a0
        g_scr[:, cs] = (cc * jax.nn.sigmoid(cc) * b).astype(BF16)

    tail = a_scr[tm:tm + SUBLANES, :]
    if short_seq:
        conv_ref[...] = a_scr[SUBLANES:SUBLANES + tm, :]
    else:
        conv_ref[...] = tail
        a_scr[0:SUBLANES, :] = tail
    out_ref[...] = x1 + _dot(g_scr[...], wdn_ref[...])


def _layer_resident(shape, layer):
    nd = len(shape)
    return pl.BlockSpec((None,) + tuple(shape), lambda *_: (layer,) + (0,) * nd,
                        pipeline_mode=pl.Buffered(1))


def _ffn(x2d, o2d, w_o, g_ffn, w_up, conv_w, conv_b, w_down, layer, seq_len, tm, prev_rows=None):
    t = x2d.shape[0]
    with_wo = o2d is not None
    short_seq = prev_rows is not None
    n_tiles = t // tm
    row = lambda i: (i, 0)
    tok = pl.BlockSpec((tm, D_MODEL), row)
    args, specs = [x2d], [tok]
    if with_wo:
        args += [o2d, w_o]
        specs += [tok, _resident((D_MODEL, D_MODEL))]
    args += [g_ffn, w_up, conv_w, conv_b, w_down]
    specs += [_layer_resident((1, D_MODEL), layer), _layer_resident((D_MODEL, 2 * D_FF), layer),
              _layer_resident((3, D_FF), layer), _layer_resident((1, D_FF), layer),
              _layer_resident((D_FF, D_MODEL), layer)]
    if short_seq:
        args += list(prev_rows)
        specs += [pl.BlockSpec((tm, D_FF), row)] * 2
        conv_shape, conv_spec = (t, D_FF), pl.BlockSpec((tm, D_FF), row)
        tiles_per_seq = 1
    else:
        conv_shape = (n_tiles, SUBLANES, D_FF)
        conv_spec = pl.BlockSpec((None, SUBLANES, D_FF), lambda i: (i, 0, 0))
        tiles_per_seq = seq_len // tm
    return pl.pallas_call(
        functools.partial(_ffn_body, with_wo=with_wo, short_seq=short_seq,
                          tiles_per_seq=tiles_per_seq, seq_len=seq_len),
        grid=(n_tiles,),
        in_specs=specs,
        out_specs=[tok, conv_spec],
        out_shape=[jax.ShapeDtypeStruct((t, D_MODEL), F32),
                   jax.ShapeDtypeStruct(conv_shape, F32)],
        scratch_shapes=[pltpu.VMEM((tm + SUBLANES, D_FF), F32), pltpu.VMEM((tm, D_FF), BF16)],
        compiler_params=pltpu.CompilerParams(
            dimension_semantics=("arbitrary",), vmem_limit_bytes=VMEM_LIMIT),
        name="conv_ffn",
    )(*args)


def _s5_body(x_ref, g_ref, win_ref, bd_ref, cd_ref, are_ref, aim_ref, dskip_ref, wglu_ref, s0_ref,
             xo_ref, sfin_ref, st_scr, *part_scr, tt):
    ti = pl.program_id(1)
    n_seq = SUBLANES
    n_parts = len(part_scr) // 3
    xt_scrs, bu_scrs, y_scrs = (part_scr[0:n_parts], part_scr[n_parts:2 * n_parts],
                                part_scr[2 * n_parts:3 * n_parts])
    tp = tt // n_parts

    @pl.when(ti == 0)
    def _():
        st_scr[...] = s0_ref[...]

    n_slabs = D_MODEL // LANES
    live = {}

    def head(p):
        xt_scr = xt_scrs[p]
        for b in range(n_seq):
            for j in range(n_slabs):
                xt_scr[j, pl.ds(b, tp, stride=n_seq), :] = (
                    x_ref[b, p * tp:(p + 1) * tp, j * LANES:(j + 1) * LANES])
        xt = jnp.concatenate([xt_scr[j] for j in range(n_slabs)], axis=1)
        u = _dot(_rms(xt, g_ref[...]).astype(BF16), win_ref[...])
        live[p] = (xt, u, u.astype(BF16))

    def blocks(p):
        bu_scr, y_scr = bu_scrs[p], y_scrs[p]
        ub = live[p][2]
        for k in range(N_SSM_BLOCKS):
            bu_scr[...] = _dot(ub[:, k * SSM_BLOCK_IN:(k + 1) * SSM_BLOCK_IN], bd_ref[k])
            for lc in range(SSM_BLOCK_STATE // SSM_SCAN_LANES):
                re_l = slice(lc * SSM_SCAN_LANES, (lc + 1) * SSM_SCAN_LANES)
                im_l = slice(SSM_BLOCK_STATE + lc * SSM_SCAN_LANES,
                             SSM_BLOCK_STATE + (lc + 1) * SSM_SCAN_LANES)
                base = k * 2 * SSM_BLOCK_STATE
                st_re = slice(base + re_l.start, base + re_l.stop)
                st_im = slice(base + im_l.start, base + im_l.stop)
                ar = jnp.broadcast_to(are_ref[k, :, re_l], (n_seq, SSM_SCAN_LANES))
                ai = jnp.broadcast_to(aim_ref[k, :, re_l], (n_seq, SSM_SCAN_LANES))

                def step(t, s, ar=ar, ai=ai, re_l=re_l, im_l=im_l, bu_scr=bu_scr):
                    sr, si = s
                    r0 = pl.multiple_of(t * n_seq, n_seq)
                    nr = ar * sr - ai * si + bu_scr[pl.ds(r0, n_seq), re_l]
                    ni = ar * si + ai * sr + bu_scr[pl.ds(r0, n_seq), im_l]
                    bu_scr[pl.ds(r0, n_seq), re_l] = nr
                    bu_scr[pl.ds(r0, n_seq), im_l] = ni
                    return nr, ni

                sr, si = lax.fori_loop(0, tp, step, (st_scr[:, st_re], st_scr[:, st_im]), unroll=True)
                st_scr[:, st_re] = sr
                st_scr[:, st_im] = si
            y_scr[:, k * SSM_BLOCK_IN:(k + 1) * SSM_BLOCK_IN] = _dot(bu_scr[...].astype(BF16), cd_ref[k])

    def tail(p):
        xt_scr, y_scr = xt_scrs[p], y_scrs[p]
        xt, u, _ = live[p]
        y = y_scr[...] + dskip_ref[...] * u
        gg = _dot(jax.nn.gelu(y).astype(BF16), wglu_ref[...])
        x_new = xt + gg[:, 0:D_MODEL] * jax.nn.sigmoid(gg[:, D_MODEL:2 * D_MODEL])
        for j in range(n_slabs):
            xt_scr[j] = x_new[:, j * LANES:(j + 1) * LANES]
        for b in range(n_seq):
            for j in range(n_slabs):
                xo_ref[b, p * tp:(p + 1) * tp, j * LANES:(j + 1) * LANES] = (
                    xt_scr[j, pl.ds(b, tp, stride=n_seq), :])

    for phase in (head, blocks, tail):
        for p in range(n_parts):
            phase(p)

    @pl.when(ti == pl.num_programs(1) - 1)
    def _():
        sfin_ref[...] = st_scr[...]


def _s5(x, g_mix, w_in, bd, cd, a_re, a_im, d_skip, w_glu, s0, tt):
    b, s, _ = x.shape
    n_state = N_SSM_BLOCKS * 2 * SSM_BLOCK_STATE
    x_spec = pl.BlockSpec((SUBLANES, tt, D_MODEL), lambda bi, ti: (bi, ti, 0))
    s_spec = pl.BlockSpec((None, SUBLANES, n_state), lambda bi, ti: (bi, 0, 0))
    n_parts = S5_PARTS if tt % (S5_PARTS * SUBLANES) == 0 else 1
    rows = tt * SUBLANES // n_parts
    return pl.pallas_call(
        functools.partial(_s5_body, tt=tt),
        grid=(b // SUBLANES, s // tt),
        in_specs=[x_spec, _resident((1, D_MODEL)), _resident((D_MODEL, D_MODEL)),
                  _resident(bd.shape), _resident(cd.shape), _resident(a_re.shape),
                  _resident(a_im.shape), _resident((1, D_MODEL)),
                  _resident((D_MODEL, 2 * D_MODEL)), s_spec],
        out_specs=[x_spec, s_spec],
        out_shape=[jax.ShapeDtypeStruct((b, s, D_MODEL), F32),
                   jax.ShapeDtypeStruct((b // SUBLANES, SUBLANES, n_state), F32)],
        scratch_shapes=([pltpu.VMEM((SUBLANES, n_state), F32)]
                        + [pltpu.VMEM((D_MODEL // LANES, rows, LANES), F32)] * n_parts
                        + [pltpu.VMEM((rows, 2 * SSM_BLOCK_STATE), F32)] * n_parts
                        + [pltpu.VMEM((rows, D_MODEL), F32)] * n_parts),
        compiler_params=pltpu.CompilerParams(
            dimension_semantics=("arbitrary", "arbitrary"), vmem_limit_bytes=VMEM_LIMIT),
        name="s5_layer",
    )(x, g_mix, w_in, bd, cd, a_re, a_im, d_skip, w_glu, s0)


def _s5_tables(lam_re, lam_im, log_dt, b_re, b_im, c_re, c_im):
    dt = jnp.exp(log_dt)[:, None]
    mag = jnp.exp(lam_re * dt)
    a_re = mag * jnp.cos(lam_im * dt)
    a_im = mag * jnp.sin(lam_im * dt)
    den = lam_re * lam_re + lam_im * lam_im
    zr = ((a_re - 1.0) * lam_re + a_im * lam_im) / den
    zi = (a_im * lam_re - (a_re - 1.0) * lam_im) / den
    bbar_re = zr[..., None] * b_re - zi[..., None] * b_im
    bbar_im = zr[..., None] * b_im + zi[..., None] * b_re
    eye = jnp.eye(SSM_GROUPS_PER_BLOCK, dtype=F32)

    def pack_in(m):
        m = m.reshape(N_SSM_BLOCKS, SSM_GROUPS_PER_BLOCK, STATE_DIM, SSM_GROUP)
        full = jnp.einsum("kgpc,gh->kgchp", m, eye)
        return full.reshape(N_SSM_BLOCKS, SSM_BLOCK_IN, SSM_BLOCK_STATE)

    def pack_out(m):
        m = m.reshape(N_SSM_BLOCKS, SSM_GROUPS_PER_BLOCK, SSM_GROUP, STATE_DIM)
        full = jnp.einsum("kgcp,gh->kgphc", m, eye)
        return full.reshape(N_SSM_BLOCKS, SSM_BLOCK_STATE, SSM_BLOCK_IN)

    bd = jnp.concatenate([pack_in(bbar_re), pack_in(bbar_im)], axis=2).astype(BF16)
    cd = jnp.concatenate([pack_out(c_re), pack_out(-c_im)], axis=1).astype(BF16)
    shape = (N_SSM_BLOCKS, 1, SSM_BLOCK_STATE)
    return bd, cd, a_re.reshape(shape), a_im.reshape(shape)


def _pack_state(s_re, s_im):
    n = s_re.shape[0]
    packed = jnp.concatenate([s_re.reshape(n, N_SSM_BLOCKS, SSM_BLOCK_STATE),
                              s_im.reshape(n, N_SSM_BLOCKS, SSM_BLOCK_STATE)], axis=2)
    return packed.reshape(n // SUBLANES, SUBLANES, -1)


def _unpack_state(packed, n):
    s = packed.reshape(n, N_SSM_BLOCKS, 2, SSM_BLOCK_STATE)
    return (s[:, :, 0].reshape(n, N_GROUPS, STATE_DIM), s[:, :, 1].reshape(n, N_GROUPS, STATE_DIM))


def _trunk(x, w, tables, *, tm, tq, tt, past=None, ssm_prev=None, conv_prev=None):
    b, s, _ = x.shape
    t = b * s
    x2d = x.reshape(t, D_MODEL)

    shape3 = (b, s, D_MODEL)
    heads = (b, s, N_HEADS, HEAD_DIM)
    if past is None:
        q, kt, vt, ktb, vtb = _qkv_t(x2d, w["norm_mix"][0], w["wq"], w["wkt"], w["wvt"], w["q_gain"],
                                     w["k_gain_col"], s, tm)
        o = _attn_prompt(q.reshape(shape3), ktb, vtb, w["bias"] * LOG2_E, -w["tri"], tq, 8)
        feat_major = lambda a: jnp.transpose(a.reshape(b, N_HEADS, HEAD_DIM, s), (0, 3, 1, 2))
        k, v = feat_major(kt), feat_major(vt)
    else:
        q, k, v = _qkv(x2d, w["norm_mix"][0], w["wq"], w["wk"], w["wv"], w["q_gain"], w["k_gain"], tm)
        cache_kt, cache_vt, page_table, pps = past
        o = _attn_sample(q.reshape(shape3), k.reshape(shape3), v.reshape(shape3), w["bias_rows"],
                         w["tri"], cache_kt, cache_vt, page_table, pps)
        k, v = k.reshape(heads), v.reshape(heads)
    prev0 = None if conv_prev is None else conv_prev[0]
    x2d, conv0 = _ffn(x2d, o.reshape(t, D_MODEL), w["w_o"], w["norm_ffn"], w["w_up"],
                      w["conv_w"], w["conv_b"], w["w_down"], 0, s, tm, prev0)

    bd, cd, a_re, a_im = tables
    if ssm_prev is None:
        s0 = jnp.zeros((b // SUBLANES, SUBLANES, N_SSM_BLOCKS * 2 * SSM_BLOCK_STATE), F32)
    else:
        s0 = _pack_state(*ssm_prev)
    x3, s_fin = _s5(x2d.reshape(shape3), w["norm_mix"][1], w["w_in"], bd, cd, a_re, a_im,
                    w["d_skip"], w["w_glu"], s0, tt)
    s_re, s_im = _unpack_state(s_fin, b)
    prev1 = None if conv_prev is None else conv_prev[1]
    y2d, conv1 = _ffn(x3.reshape(t, D_MODEL), None, None, w["norm_ffn"], w["w_up"],
                      w["conv_w"], w["conv_b"], w["w_down"], 1, s, tm, prev1)

    def conv_state(c):
        if conv_prev is None:
            per_seq = c.reshape(b, s // tm, SUBLANES, D_FF)[:, -1]
        else:
            per_seq = c.reshape(b, s, D_FF)
        return per_seq[:, -2:]

    return (y2d.reshape(shape3), k[None], v[None], s_re[None], s_im[None],
            jnp.stack([conv_state(conv0), conv_state(conv1)]))


def _short_seq_prev_rows(prev, seq_len):
    bsz = prev.shape[0]
    zeros = jnp.zeros((bsz, seq_len - 2, D_FF), F32)
    p2 = jnp.concatenate([prev, zeros], axis=1)
    p1 = jnp.concatenate([prev[:, 1:2], jnp.zeros((bsz, seq_len - 1, D_FF), F32)], axis=1)
    return p1.reshape(bsz * seq_len, D_FF), p2.reshape(bsz * seq_len, D_FF)


def kernel(x_prompt, x_sample, cache_k, cache_v, state_ssm_re, state_ssm_im, state_ffn_conv, page_table,
           norm_mix, norm_ffn, attn_w_qkv, attn_q_gain, attn_k_gain, attn_logit_bias, attn_w_o,
           ssm_w_in, ssm_lambda_re, ssm_lambda_im, ssm_log_dt, ssm_b_re, ssm_b_im, ssm_c_re, ssm_c_im,
           ssm_d, ssm_w_glu, ffn_w_up, ffn_conv_w, ffn_conv_b, ffn_w_down):
    dec_batch, dec_seq, _ = x_sample.shape
    w = {
        "norm_mix": norm_mix[:, None, :], "norm_ffn": norm_ffn[:, None, :],
        "wq": attn_w_qkv[0, :, 0:D_MODEL].astype(BF16),
        "wk": attn_w_qkv[0, :, D_MODEL:2 * D_MODEL].astype(BF16),
        "wv": attn_w_qkv[0, :, 2 * D_MODEL:3 * D_MODEL].astype(BF16),
        "wkt": attn_w_qkv[0, :, D_MODEL:2 * D_MODEL].T.astype(BF16),
        "wvt": attn_w_qkv[0, :, 2 * D_MODEL:3 * D_MODEL].T.astype(BF16),
        "q_gain": jnp.tile(attn_q_gain[0], N_HEADS)[None], "k_gain": jnp.tile(attn_k_gain[0], N_HEADS)[None],
        "k_gain_col": attn_k_gain[0][:, None],
        "bias": attn_logit_bias[0],
        "bias_rows": jnp.repeat(attn_logit_bias[0], dec_seq)[:, None],
        "tri": jnp.tril(jnp.ones((MXU_DIM, MXU_DIM), BF16)),
        "w_o": attn_w_o[0].astype(BF16),
        "w_in": ssm_w_in[0].astype(BF16), "d_skip": ssm_d[0][None], "w_glu": ssm_w_glu[0].astype(BF16),
        "w_up": ffn_w_up.astype(BF16), "conv_w": ffn_conv_w, "conv_b": ffn_conv_b[:, None, :],
        "w_down": ffn_w_down.astype(BF16),
    }
    tables = _s5_tables(ssm_lambda_re[0], ssm_lambda_im[0], ssm_log_dt[0], ssm_b_re[0], ssm_b_im[0],
                        ssm_c_re[0], ssm_c_im[0])

    y_p, k_p, v_p, sr_p, si_p, conv_p = _trunk(x_prompt, w, tables, tm=512, tq=256, tt=64)

    n_pool = cache_k.shape[1]
    to_pages = lambda c: jnp.transpose(c[0], (0, 2, 3, 1)).reshape(n_pool, D_MODEL, PAGE_SIZE)
    past = (to_pages(cache_k), to_pages(cache_v), page_table, 16)
    conv_prev = [_short_seq_prev_rows(state_ffn_conv[i], dec_seq) for i in range(state_ffn_conv.shape[0])]
    y_s, k_s, v_s, sr_s, si_s, conv_s = _trunk(
        x_sample, w, tables, tm=dec_batch * dec_seq, tq=None, tt=dec_seq,
        past=past, ssm_prev=(state_ssm_re[0], state_ssm_im[0]), conv_prev=conv_prev)

    return (y_p, y_s, k_p, v_p, k_s, v_s, sr_p, si_p, sr_s, si_s, conv_p, conv_s)
```
